```python
import jax, jax.numpy as jnp
from jax import lax
import numpy as np

D_MODEL = 1024
BATCH = 2
SEQ = 16384
DEPTH = 2

GRID_W = 64
FNET_GROUPS = 4
FNET_GROUP_DIM = 128
FNET_WIDTH = FNET_GROUPS * FNET_GROUP_DIM
NA_HEADS = 8
NA_HEAD_DIM = 64
NA_WIDTH = NA_HEADS * NA_HEAD_DIM
NA_ROW_WIN = 8
NA_COL_WIN = 16
RPB_ROWS = 2 * NA_ROW_WIN - 1
RPB_COLS = 2 * NA_COL_WIN - 1
N_BRANCHES = 2
IN_WIDTH = FNET_WIDTH + 3 * NA_WIDTH + N_BRANCHES * D_MODEL
D_FF = 2816
CONV_W = 3
RMS_EPS = 1e-6

kernel_name = "fnet_natten_gated_hybrid_encoder"


def rms_norm(x, g):
    xf = x.astype(jnp.float32)
    y = xf * lax.rsqrt(jnp.mean(xf * xf, axis=-1, keepdims=True) + RMS_EPS)
    return (y * g.astype(jnp.float32)).astype(x.dtype)


def fourier_mix(u):
    b, s, _ = u.shape
    ug = u.reshape(b, s, FNET_GROUPS, FNET_GROUP_DIM).astype(jnp.float32)
    f = jnp.fft.fft2(ug, axes=(1, 3), norm="ortho").real
    return f.reshape(b, s, FNET_WIDTH).astype(u.dtype)


def neighbourhood_attention(q, k, v, rpb):
    b, s, _ = q.shape
    rows = s // GRID_W
    kr = min(NA_ROW_WIN, rows)
    shp = (b, rows, GRID_W, NA_HEADS, NA_HEAD_DIM)
    q = q.reshape(shp) * (NA_HEAD_DIM ** -0.5)
    k = k.reshape(shp)
    v = v.reshape(shp)
    cols = jnp.arange(GRID_W)
    col_start = jnp.clip(cols - NA_COL_WIN // 2, 0, GRID_W - NA_COL_WIN)
    col_idx = col_start[:, None] + jnp.arange(NA_COL_WIN)[None, :]
    dc = col_idx - cols[:, None] + (NA_COL_WIN - 1)
    rpb32 = rpb.astype(jnp.float32)

    def row_block(r):
        rs = jnp.clip(r - kr // 2, 0, rows - kr)
        q_r = lax.dynamic_index_in_dim(q, r, axis=1, keepdims=False)
        k_r = lax.dynamic_slice_in_dim(k, rs, kr, axis=1)[:, :, col_idx]
        v_r = lax.dynamic_slice_in_dim(v, rs, kr, axis=1)[:, :, col_idx]
        dr = rs + jnp.arange(kr) - r + (NA_ROW_WIN - 1)
        bias = rpb32[:, dr[None, :, None], dc[:, None, :]]
        sc = jnp.einsum('bqhd,brqwhd->bhqrw', q_r, k_r).astype(jnp.float32) + bias[None]
        p = jax.nn.softmax(sc.reshape(b, NA_HEADS, GRID_W, kr * NA_COL_WIN), axis=-1)
        p = p.reshape(b, NA_HEADS, GRID_W, kr, NA_COL_WIN).astype(v.dtype)
        return jnp.einsum('bhqrw,brqwhd->bqhd', p, v_r)

    out = lax.map(row_block, jnp.arange(rows))
    return jnp.transpose(out, (1, 0, 2, 3, 4)).reshape(b, s, NA_WIDTH)


def depthwise_conv_seq(u, w, bias):
    s = u.shape[1]
    pad = CONV_W // 2
    up = jnp.pad(u, ((0, 0), (pad, CONV_W - 1 - pad), (0, 0)))
    out = bias
    for j in range(CONV_W):
        out = out + up[:, j:j + s] * w[j]
    return out


def hybrid_layer(x, norm1_g, w_in, b_gate, rpb, w_fa, w_nb, w_out,
                 norm2_g, w_up, conv_w, conv_b, w_down):
    b, s, _ = x.shape
    h = rms_norm(x, norm1_g)
    proj = h @ w_in
    o1 = FNET_WIDTH
    o2 = o1 + NA_WIDTH
    o3 = o2 + NA_WIDTH
    o4 = o3 + NA_WIDTH
    u_f, q, k, v, g = jnp.split(proj, [o1, o2, o3, o4], axis=-1)
    g = jax.nn.sigmoid(g.reshape(b, s, N_BRANCHES, D_MODEL) + b_gate)
    a_branch = fourier_mix(u_f) @ w_fa
    n_branch = neighbourhood_attention(q, k, v, rpb) @ w_nb
    mixed = g[:, :, 0] * a_branch + g[:, :, 1] * n_branch
    x = x + mixed @ w_out
    h2 = rms_norm(x, norm2_g)
    up = depthwise_conv_seq(h2 @ w_up, conv_w, conv_b)
    val, gate = jnp.split(up, 2, axis=-1)
    x = x + (jax.nn.silu(gate) * val) @ w_down
    return x


def setup_inputs(seed: int = 0) -> dict:
    key = jax.random.key(seed)
    ks = jax.random.split(key, 16)
    f32 = jnp.float32
    L, D = DEPTH, D_MODEL
    nrm = lambda k, shape, scale: jax.random.normal(k, shape, f32) * scale
    return {
        "x": jax.random.normal(ks[0], (BATCH, SEQ, D), f32),
        "norm1_g": 1.0 + nrm(ks[1], (L, D), 0.05),
        "w_in": nrm(ks[2], (L, D, IN_WIDTH), D ** -0.5),
        "b_gate": nrm(ks[3], (L, N_BRANCHES, D), 0.1),
        "rpb": nrm(ks[4], (L, NA_HEADS, RPB_ROWS, RPB_COLS), 0.5),
        "w_fa": nrm(ks[5], (L, FNET_WIDTH, D), FNET_WIDTH ** -0.5),
        "w_nb": nrm(ks[6], (L, NA_WIDTH, D), NA_WIDTH ** -0.5),
        "w_out": nrm(ks[7], (L, D, D), D ** -0.5),
        "norm2_g": 1.0 + nrm(ks[8], (L, D), 0.05),
        "w_up": nrm(ks[9], (L, D, 2 * D_FF), D ** -0.5),
        "conv_w": nrm(ks[10], (L, CONV_W, 2 * D_FF), CONV_W ** -0.5),
        "conv_b": nrm(ks[11], (L, 2 * D_FF), 0.02),
        "w_down": nrm(ks[12], (L, D_FF, D), D_FF ** -0.5),
        "final_g": 1.0 + nrm(ks[13], (D,), 0.05),
    }


def reference(x, norm1_g, w_in, b_gate, rpb, w_fa, w_nb, w_out,
              norm2_g, w_up, conv_w, conv_b, w_down, final_g):
    for l in range(DEPTH):
        x = hybrid_layer(x, norm1_g[l], w_in[l], b_gate[l], rpb[l], w_fa[l], w_nb[l], w_out[l],
                         norm2_g[l], w_up[l], conv_w[l], conv_b[l], w_down[l])
    return rms_norm(x, final_g)
```

```python
import functools
import math

import jax
import jax.numpy as jnp
from jax import lax
from jax.experimental import pallas as pl
from jax.experimental.pallas import tpu as pltpu

F32 = jnp.float32
BF16 = jnp.bfloat16

D_MODEL = 1024
GRID_W = 64
FNET_GROUPS = 4
FNET_GROUP_DIM = 128
FNET_WIDTH = FNET_GROUPS * FNET_GROUP_DIM
NA_HEADS = 8
NA_HEAD_DIM = 64
NA_WIDTH = NA_HEADS * NA_HEAD_DIM
NA_ROW_WIN = 8
NA_COL_WIN = 16
D_FF = 2816
RMS_EPS = 1e-6

FFT_RADIX = 128
LOG2E = 1.4426950408889634
MASK_VALUE = -1e30
HEAD_PAIRS = NA_HEADS // 2
PAIR_W = 2 * NA_HEAD_DIM
ROW_TOKENS = GRID_W
QROWS = 8
QBLOCK = QROWS * ROW_TOKENS
WIN_TOKENS = NA_ROW_WIN * ROW_TOKENS

VMEM_LIMIT = 56 * 1024 * 1024


def _cparams(sem):
    return pltpu.CompilerParams(dimension_semantics=sem, vmem_limit_bytes=VMEM_LIMIT)


def _rms(x, g):
    ms = jnp.mean(x * x, axis=-1, keepdims=True)
    return x * lax.rsqrt(ms + RMS_EPS) * g


def _sigmoid(x):
    return 1.0 / (1.0 + jnp.exp(-x))


def _inproj_kernel(x_ref, g1_ref, w_ref, bg_ref, wch_ref,
                   zr_ref, zi_ref, q_ref, k_ref, v_ref, gate_ref):
    h = _rms(x_ref[...], g1_ref[...]).astype(BF16)

    def proj(lo, hi):
        return jnp.dot(h, w_ref[:, lo:hi], preferred_element_type=F32)

    u = proj(0, FNET_WIDTH).astype(BF16)
    for g in range(FNET_GROUPS):
        sl = slice(g * FNET_GROUP_DIM, (g + 1) * FNET_GROUP_DIM)
        z = jnp.dot(u[:, sl], wch_ref[...], preferred_element_type=F32)
        zr_ref[:, sl] = z[:, :FNET_GROUP_DIM].astype(BF16)
        zi_ref[:, sl] = z[:, FNET_GROUP_DIM:].astype(BF16)
    o = FNET_WIDTH
    q_ref[...] = (proj(o, o + NA_WIDTH) * (NA_HEAD_DIM ** -0.5 * LOG2E)).astype(BF16)
    k_ref[...] = proj(o + NA_WIDTH, o + 2 * NA_WIDTH).astype(BF16)
    v_ref[...] = proj(o + 2 * NA_WIDTH, o + 3 * NA_WIDTH).astype(BF16)
    o += 3 * NA_WIDTH
    for j in range(2):
        gj = proj(o + j * D_MODEL, o + (j + 1) * D_MODEL) + bg_ref[j:j + 1, :]
        gate_ref[:, j * D_MODEL:(j + 1) * D_MODEL] = _sigmoid(gj).astype(BF16)


def _inproj(x2, g1, w_in, b_gate, wch, tm):
    n = x2.shape[0]
    in_w = w_in.shape[1]
    tok = lambda w: pl.BlockSpec((tm, w), lambda i: (i, 0))
    full = lambda a: pl.BlockSpec(a.shape, lambda i: (0,) * a.ndim)
    outs = [jax.ShapeDtypeStruct((n, FNET_WIDTH), BF16)] * 2 + \
           [jax.ShapeDtypeStruct((n, NA_WIDTH), BF16)] * 3 + \
           [jax.ShapeDtypeStruct((n, 2 * D_MODEL), BF16)]
    return pl.pallas_call(
        _inproj_kernel,
        grid=(n // tm,),
        in_specs=[tok(D_MODEL), full(g1), full(w_in), full(b_gate), full(wch)],
        out_specs=[tok(FNET_WIDTH)] * 2 + [tok(NA_WIDTH)] * 3 + [tok(2 * D_MODEL)],
        out_shape=outs,
        compiler_params=_cparams(("parallel",)),
        name="inproj",
    )(x2, g1, w_in, b_gate, wch)


def _fft1_kernel(zr_ref, zi_ref, f_ref, t_ref):
    rhs = jnp.concatenate([zr_ref[...], zi_ref[...]], axis=0)
    t_ref[...] = jnp.dot(f_ref[...], rhs, preferred_element_type=F32).astype(BF16)


def _fft1(zr2, zi2, f1, nb):
    b, r, n = zr2.shape
    zspec = pl.BlockSpec((None, r, nb), lambda bi, j: (bi, 0, j))
    return pl.pallas_call(
        _fft1_kernel,
        grid=(b, n // nb),
        in_specs=[zspec, zspec, pl.BlockSpec(f1.shape, lambda bi, j: (0, 0))],
        out_specs=pl.BlockSpec((None, 2 * r, nb), lambda bi, j: (bi, 0, j)),
        out_shape=jax.ShapeDtypeStruct((b, 2 * r, n), BF16),
        compiler_params=_cparams(("parallel", "parallel")),
        name="fft1",
    )(zr2, zi2, f1)


def _fft2_kernel(t_ref, g_ref, y_ref, *, kb):
    for kk in range(kb):
        rhs = jnp.concatenate([t_ref[0, kk], t_ref[1, kk]], axis=0)
        y_ref[kk] = jnp.dot(g_ref[kk], rhs, preferred_element_type=F32).astype(BF16)


def _fft2(t5, gtab, kb):
    b, _, r, _, c = t5.shape
    return pl.pallas_call(
        functools.partial(_fft2_kernel, kb=kb),
        grid=(b, r // kb),
        in_specs=[pl.BlockSpec((None, 2, kb, r, c), lambda bi, j: (bi, 0, j, 0, 0)),
                  pl.BlockSpec((kb, r, 2 * r), lambda bi, j: (j, 0, 0))],
        out_specs=pl.BlockSpec((None, kb, r, c), lambda bi, j: (bi, j, 0, 0)),
        out_shape=jax.ShapeDtypeStruct((b, r, r, c), BF16),
        compiler_params=_cparams(("parallel", "parallel")),
        name="fft2",
    )(t5, gtab)


def _dft_tables():
    r = FFT_RADIX
    idx = jnp.arange(r, dtype=jnp.int32)
    scale = r ** -0.5
    ang = (idx[:, None] * idx[None, :]) % r
    ang = ang.astype(F32) * (2.0 * math.pi / r)
    c, s = jnp.cos(ang) * scale, jnp.sin(ang) * scale
    wch = jnp.concatenate([c, -s], axis=1).astype(BF16)
    f1 = jnp.concatenate([jnp.concatenate([c, s], axis=1),
                          jnp.concatenate([-s, c], axis=1)], axis=0).astype(BF16)
    k1 = idx[:, None, None]
    k2 = idx[None, :, None]
    bb = idx[None, None, :]
    m = (bb * (r * k2 + k1)) % (r * r)
    th = m.astype(F32) * (2.0 * math.pi / (r * r))
    gtab = jnp.concatenate([jnp.cos(th) * scale, jnp.sin(th) * scale], axis=2).astype(BF16)
    return wch, f1, gtab


def _natten_kernel(q_ref, kp_ref, kc_ref, kn_ref, vp_ref, vc_ref, vn_ref, bias_ref,
                   o_ref, kbuf, vbuf, *, rows):
    i = pl.program_id(1)
    kbuf[0:QBLOCK] = kp_ref[...]
    kbuf[QBLOCK:2 * QBLOCK] = kc_ref[...]
    kbuf[2 * QBLOCK:3 * QBLOCK] = kn_ref[...]
    vbuf[0:QBLOCK] = vp_ref[...]
    vbuf[QBLOCK:2 * QBLOCK] = vc_ref[...]
    vbuf[2 * QBLOCK:3 * QBLOCK] = vn_ref[...]
    lane = lax.broadcasted_iota(jnp.int32, (ROW_TOKENS, PAIR_W), 1)
    lo_half = lane < NA_HEAD_DIM

    def row_body(j, carry):
        r = i * QROWS + j
        rs = jnp.clip(r - NA_ROW_WIN // 2, 0, rows - NA_ROW_WIN)
        delta = r - rs
        off = pl.multiple_of((rs - (i - 1) * QROWS) * ROW_TOKENS, ROW_TOKENS)
        qoff = pl.multiple_of(j * ROW_TOKENS, ROW_TOKENS)
        for t in range(HEAD_PAIRS):
            sl = slice(t * PAIR_W, (t + 1) * PAIR_W)
            q2 = q_ref[pl.ds(qoff, ROW_TOKENS), sl]
            zero = jnp.zeros_like(q2)
            qq = jnp.concatenate([jnp.where(lo_half, q2, zero),
                                  jnp.where(lo_half, zero, q2)], axis=0)
            kw = kbuf[pl.ds(off, WIN_TOKENS), sl]
            vw = vbuf[pl.ds(off, WIN_TOKENS), sl]
            s = lax.dot_general(qq, kw, (((1,), (1,)), ((), ())),
                                preferred_element_type=F32)
            s = s + bias_ref[delta, t]
            m = jnp.max(s, axis=-1, keepdims=True)
            p = jnp.exp2(s - m)
            l = jnp.sum(p, axis=-1, keepdims=True)
            o = jnp.dot(p.astype(BF16), vw, preferred_element_type=F32) / l
            o2 = jnp.where(lo_half, o[:ROW_TOKENS], o[ROW_TOKENS:])
            o_ref[pl.ds(qoff, ROW_TOKENS), sl] = o2.astype(BF16)
        return carry

    lax.fori_loop(0, QROWS, row_body, 0)


def _natten(q, k, v, bias_tab):
    b, s, w = q.shape
    rows = s // GRID_W
    nblk = rows // QROWS
    blk = (None, QBLOCK, w)
    cur = pl.BlockSpec(blk, lambda bi, i: (bi, i, 0))
    prv = pl.BlockSpec(blk, lambda bi, i: (bi, jnp.maximum(i - 1, 0), 0))
    nxt = pl.BlockSpec(blk, lambda bi, i: (bi, jnp.minimum(i + 1, nblk - 1), 0))
    return pl.pallas_call(
        functools.partial(_natten_kernel, rows=rows),
        grid=(b, nblk),
        in_specs=[cur, prv, cur, nxt, prv, cur, nxt,
                  pl.BlockSpec(bias_tab.shape, lambda bi, i: (0, 0, 0, 0))],
        out_specs=cur,
        out_shape=jax.ShapeDtypeStruct((b, s, w), BF16),
        scratch_shapes=[pltpu.VMEM((3 * QBLOCK, w), BF16), pltpu.VMEM((3 * QBLOCK, w), BF16)],
        compiler_params=_cparams(("parallel", "parallel")),
        name="natten",
    )(q, k, k, k, v, v, v, bias_tab)


def _bias_table(rpb):
    cols = jnp.arange(GRID_W)
    col_start = jnp.clip(cols - NA_COL_WIN // 2, 0, GRID_W - NA_COL_WIN)
    kc = cols[None, :]
    valid = (kc >= col_start[:, None]) & (kc < col_start[:, None] + NA_COL_WIN)
    dc = jnp.clip(kc - cols[:, None] + (NA_COL_WIN - 1), 0, 2 * NA_COL_WIN - 2)
    delta = jnp.arange(NA_ROW_WIN)[:, None]
    wi = jnp.arange(NA_ROW_WIN)[None, :]
    dr = wi + (NA_ROW_WIN - 1) - delta
    tab = rpb.astype(F32)[:, dr[:, :, None, None], dc[None, None, :, :]]
    tab = jnp.where(valid[None, None, None], tab * LOG2E, MASK_VALUE)
    tab = jnp.transpose(tab, (1, 0, 3, 2, 4))
    return tab.reshape(NA_ROW_WIN, HEAD_PAIRS, 2 * GRID_W, WIN_TOKENS)


def _merge_kernel(x_ref, y_ref, a_ref, gate_ref, wfa_ref, wnb_ref, wout_ref, o_ref):
    fa = jnp.dot(y_ref[...], wfa_ref[...], preferred_element_type=F32)
    nb = jnp.dot(a_ref[...], wnb_ref[...], preferred_element_type=F32)
    mixed = gate_ref[:, :D_MODEL].astype(F32) * fa + gate_ref[:, D_MODEL:].astype(F32) * nb
    o_ref[...] = x_ref[...] + jnp.dot(mixed.astype(BF16), wout_ref[...],
                                      preferred_element_type=F32)


def _merge(x2, y, att, gates, w_fa, w_nb, w_out, tm):
    n = x2.shape[0]
    tok = lambda w: pl.BlockSpec((tm, w), lambda i: (i, 0))
    full = lambda a: pl.BlockSpec(a.shape, lambda i: (0,) * a.ndim)
    return pl.pallas_call(
        _merge_kernel,
        grid=(n // tm,),
        in_specs=[tok(D_MODEL), tok(FNET_WIDTH), tok(NA_WIDTH), tok(2 * D_MODEL),
                  full(w_fa), full(w_nb), full(w_out)],
        out_specs=tok(D_MODEL),
        out_shape=jax.ShapeDtypeStruct((n, D_MODEL), F32),
        compiler_params=_cparams(("parallel",)),
        name="merge",
    )(x2, y, att, gates, w_fa, w_nb, w_out)


HALO = 8
FF_CHUNK = 1408


def _ffn_kernel(xp_ref, xc_ref, xn_ref, g2_ref, wup_ref, cw_ref, cb_ref, wdn_ref, gf_ref,
                o_ref, up_scr, *, tm, tiles_per_seq, final_norm):
    t = pl.program_id(0) % tiles_per_seq
    xc = xc_ref[...]
    xp = jnp.where(t > 0, xp_ref[...], 0.0)
    xn = jnp.where(t < tiles_per_seq - 1, xn_ref[...], 0.0)
    xin = jnp.concatenate([xp, xc, xn], axis=0)
    h = _rms(xin, g2_ref[...]).astype(BF16)
    acc = xc
    for c in range(D_FF // FF_CHUNK):
        act = None
        for part in range(2):
            lo = part * D_FF + c * FF_CHUNK
            up_scr[...] = jnp.dot(h, wup_ref[:, lo:lo + FF_CHUNK], preferred_element_type=F32)
            conv = cb_ref[:, lo:lo + FF_CHUNK]
            for tap in range(3):
                conv = conv + up_scr[HALO - 1 + tap:HALO - 1 + tap + tm, :] * \
                    cw_ref[tap:tap + 1, lo:lo + FF_CHUNK]
            act = conv if part == 0 else act * (conv * _sigmoid(conv))
        acc = acc + jnp.dot(act.astype(BF16), wdn_ref[c * FF_CHUNK:(c + 1) * FF_CHUNK, :],
                            preferred_element_type=F32)
    if final_norm:
        acc = _rms(acc, gf_ref[...])
    o_ref[...] = acc


def _ffn(x2, g2, w_up, conv_w, conv_b, w_down, gf, tm, seq, final_norm):
    n = x2.shape[0]
    hb = tm // HALO
    nhalo = n // HALO
    full = lambda a: pl.BlockSpec(a.shape, lambda i: (0,) * a.ndim)
    return pl.pallas_call(
        functools.partial(_ffn_kernel, tm=tm, tiles_per_seq=seq // tm, final_norm=final_norm),
        grid=(n // tm,),
        in_specs=[pl.BlockSpec((HALO, D_MODEL), lambda i: (jnp.maximum(i * hb - 1, 0), 0)),
                  pl.BlockSpec((tm, D_MODEL), lambda i: (i, 0)),
                  pl.BlockSpec((HALO, D_MODEL), lambda i: (jnp.minimum((i + 1) * hb, nhalo - 1), 0)),
                  full(g2), full(w_up), full(conv_w), full(conv_b), full(w_down), full(gf)],
        out_specs=pl.BlockSpec((tm, D_MODEL), lambda i: (i, 0)),
        out_shape=jax.ShapeDtypeStruct((n, D_MODEL), F32),
        scratch_shapes=[pltpu.VMEM((tm + 2 * HALO, FF_CHUNK), F32)],
        compiler_params=_cparams(("parallel",)),
        name="ffn",
    )(x2, x2, x2, g2, w_up, conv_w, conv_b, w_down, gf)


def kernel(x, norm1_g, w_in, b_gate, rpb, w_fa, w_nb, w_out, norm2_g, w_up, conv_w, conv_b,
           w_down, final_g):
    b, s, d = x.shape
    depth = w_in.shape[0]
    r = FFT_RADIX
    assert s == r * r and d == D_MODEL and s % (GRID_W * QROWS) == 0
    wch, f1, gtab = _dft_tables()
    x2 = x.reshape(b * s, d)
    gf = final_g.reshape(1, d)
    for l in range(depth):
        zr, zi, q, k, v, gates = _inproj(
            x2, norm1_g[l].reshape(1, d), w_in[l].astype(BF16), b_gate[l], wch, tm=512)
        t = _fft1(zr.reshape(b, r, r * FNET_WIDTH), zi.reshape(b, r, r * FNET_WIDTH), f1, nb=4096)
        yt = _fft2(t.reshape(b, 2, r, r, FNET_WIDTH), gtab, kb=8)
        y = jnp.transpose(yt, (0, 2, 1, 3)).reshape(b * s, FNET_WIDTH)
        att = _natten(q.reshape(b, s, NA_WIDTH), k.reshape(b, s, NA_WIDTH),
                      v.reshape(b, s, NA_WIDTH), _bias_table(rpb[l]))
        x2 = _merge(x2, y, att.reshape(b * s, NA_WIDTH), gates,
                    w_fa[l].astype(BF16), w_nb[l].astype(BF16), w_out[l].astype(BF16), tm=512)
        x2 = _ffn(x2, norm2_g[l].reshape(1, d), w_up[l].astype(BF16), conv_w[l],
                  conv_b[l].reshape(1, 2 * D_FF), w_down[l].astype(BF16), gf,
                  tm=512, seq=s, final_norm=(l == depth - 1))
    return x2.reshape(b, s, d)
```

```python
import functools
import math

import jax
import jax.numpy as jnp
import numpy as np
from jax import lax
from jax.experimental import pallas as pl
from jax.experimental.pallas import tpu as pltpu

F32 = jnp.float32
BF16 = jnp.bfloat16

D_MODEL = 1024
GRID_W = 64
FNET_GROUPS = 4
FNET_GROUP_DIM = 128
FNET_WIDTH = FNET_GROUPS * FNET_GROUP_DIM
NA_HEADS = 8
NA_HEAD_DIM = 64
NA_WIDTH = NA_HEADS * NA_HEAD_DIM
NA_ROW_WIN = 8
NA_COL_WIN = 16
D_FF = 2816
RMS_EPS = 1e-6

FFT_RADIX = 128
LOG2E = 1.4426950408889634
MASK_VALUE = -1e30
HEAD_PAIRS = NA_HEADS // 2
PAIR_W = 2 * NA_HEAD_DIM
ROW_TOKENS = GRID_W
QROWS = 8
QBLOCK = QROWS * ROW_TOKENS
WIN_TOKENS = NA_ROW_WIN * ROW_TOKENS

VMEM_LIMIT = 56 * 1024 * 1024


def _cparams(sem):
    return pltpu.CompilerParams(dimension_semantics=sem, vmem_limit_bytes=VMEM_LIMIT)


def _rms(x, g):
    ms = jnp.mean(x * x, axis=-1, keepdims=True)
    return x * lax.rsqrt(ms + RMS_EPS) * g


def _sigmoid(x):
    return 1.0 / (1.0 + jnp.exp(-x))


def _inproj_kernel(x_ref, g1_ref, w_ref, bg_ref, wch_ref,
                   zr_ref, zi_ref, q_ref, k_ref, v_ref, gate_ref):
    h = _rms(x_ref[...], g1_ref[...]).astype(BF16)

    def proj(lo, hi):
        return jnp.dot(h, w_ref[:, lo:hi], preferred_element_type=F32)

    u = proj(0, FNET_WIDTH).astype(BF16)
    for g in range(FNET_GROUPS):
        sl = slice(g * FNET_GROUP_DIM, (g + 1) * FNET_GROUP_DIM)
        z = jnp.dot(u[:, sl], wch_ref[...], preferred_element_type=F32)
        zr_ref[:, sl] = z[:, :FNET_GROUP_DIM].astype(BF16)
        zi_ref[:, sl] = z[:, FNET_GROUP_DIM:].astype(BF16)
    o = FNET_WIDTH
    q_ref[...] = (proj(o, o + NA_WIDTH) * (NA_HEAD_DIM ** -0.5 * LOG2E)).astype(BF16)
    k_ref[...] = proj(o + NA_WIDTH, o + 2 * NA_WIDTH).astype(BF16)
    v_ref[...] = proj(o + 2 * NA_WIDTH, o + 3 * NA_WIDTH).astype(BF16)
    o += 3 * NA_WIDTH
    for j in range(2):
        gj = proj(o + j * D_MODEL, o + (j + 1) * D_MODEL) + bg_ref[j:j + 1, :]
        gate_ref[:, j * D_MODEL:(j + 1) * D_MODEL] = _sigmoid(gj).astype(BF16)


def _inproj(x2, g1, w_in, b_gate, wch, tm):
    n = x2.shape[0]
    in_w = w_in.shape[1]
    tok = lambda w: pl.BlockSpec((tm, w), lambda i: (i, 0))
    full = lambda a: pl.BlockSpec(a.shape, lambda i: (0,) * a.ndim)
    outs = [jax.ShapeDtypeStruct((n, FNET_WIDTH), BF16)] * 2 + \
           [jax.ShapeDtypeStruct((n, NA_WIDTH), BF16)] * 3 + \
           [jax.ShapeDtypeStruct((n, 2 * D_MODEL), BF16)]
    return pl.pallas_call(
        _inproj_kernel,
        grid=(n // tm,),
        in_specs=[tok(D_MODEL), full(g1), full(w_in), full(b_gate), full(wch)],
        out_specs=[tok(FNET_WIDTH)] * 2 + [tok(NA_WIDTH)] * 3 + [tok(2 * D_MODEL)],
        out_shape=outs,
        compiler_params=_cparams(("parallel",)),
        name="inproj",
    )(x2, g1, w_in, b_gate, wch)


def _fft1_kernel(zr_ref, zi_ref, f_ref, t_ref):
    rhs = jnp.concatenate([zr_ref[...], zi_ref[...]], axis=0)
    t_ref[...] = jnp.dot(f_ref[...], rhs, preferred_element_type=F32).astype(BF16)


def _fft1(zr2, zi2, f1, nb):
    b, r, n = zr2.shape
    zspec = pl.BlockSpec((None, r, nb), lambda bi, j: (bi, 0, j))
    return pl.pallas_call(
        _fft1_kernel,
        grid=(b, n // nb),
        in_specs=[zspec, zspec, pl.BlockSpec(f1.shape, lambda bi, j: (0, 0))],
        out_specs=pl.BlockSpec((None, 2 * r, nb), lambda bi, j: (bi, 0, j)),
        out_shape=jax.ShapeDtypeStruct((b, 2 * r, n), BF16),
        compiler_params=_cparams(("parallel", "parallel")),
        name="fft1",
    )(zr2, zi2, f1)


def _fft2_kernel(t_ref, g_ref, y_ref, *, kb):
    for kk in range(kb):
        rhs = jnp.concatenate([t_ref[0, kk], t_ref[1, kk]], axis=0)
        y_ref[kk] = jnp.dot(g_ref[kk], rhs, preferred_element_type=F32).astype(BF16)


def _fft2(t5, gtab, kb):
    b, _, r, _, c = t5.shape
    return pl.pallas_call(
        functools.partial(_fft2_kernel, kb=kb),
        grid=(b, r // kb),
        in_specs=[pl.BlockSpec((None, 2, kb, r, c), lambda bi, j: (bi, 0, j, 0, 0)),
                  pl.BlockSpec((kb, r, 2 * r), lambda bi, j: (j, 0, 0))],
        out_specs=pl.BlockSpec((None, kb, r, c), lambda bi, j: (bi, j, 0, 0)),
        out_shape=jax.ShapeDtypeStruct((b, r, r, c), BF16),
        compiler_params=_cparams(("parallel", "parallel")),
        name="fft2",
    )(t5, gtab)


@functools.lru_cache(maxsize=None)
def _dft_tables_np():
    r = FFT_RADIX
    idx = np.arange(r, dtype=np.int64)
    scale = r ** -0.5
    ang = ((idx[:, None] * idx[None, :]) % r) * (2.0 * math.pi / r)
    c, s = np.cos(ang) * scale, np.sin(ang) * scale
    wch = np.concatenate([c, -s], axis=1)
    f1 = np.concatenate([np.concatenate([c, s], axis=1),
                         np.concatenate([-s, c], axis=1)], axis=0)
    k1 = idx[:, None, None]
    k2 = idx[None, :, None]
    bb = idx[None, None, :]
    th = ((bb * (r * k2 + k1)) % (r * r)) * (2.0 * math.pi / (r * r))
    gtab = np.concatenate([np.cos(th) * scale, np.sin(th) * scale], axis=2)
    return tuple(a.astype(np.float32) for a in (wch, f1, gtab))


def _dft_tables():
    return tuple(jnp.asarray(a).astype(BF16) for a in _dft_tables_np())


def _natten_kernel(q_ref, kp_ref, kc_ref, kn_ref, vp_ref, vc_ref, vn_ref, bias_ref,
                   o_ref, kbuf, vbuf, *, rows):
    i = pl.program_id(1)
    kbuf[0:QBLOCK] = kp_ref[...]
    kbuf[QBLOCK:2 * QBLOCK] = kc_ref[...]
    kbuf[2 * QBLOCK:3 * QBLOCK] = kn_ref[...]
    vbuf[0:QBLOCK] = vp_ref[...]
    vbuf[QBLOCK:2 * QBLOCK] = vc_ref[...]
    vbuf[2 * QBLOCK:3 * QBLOCK] = vn_ref[...]
    lane = lax.broadcasted_iota(jnp.int32, (ROW_TOKENS, PAIR_W), 1)
    lo_half = lane < NA_HEAD_DIM

    deltas, offs = [], []
    for j in range(QROWS):
        r = i * QROWS + j
        rs = jnp.clip(r - NA_ROW_WIN // 2, 0, rows - NA_ROW_WIN)
        deltas.append(r - rs)
        offs.append(pl.multiple_of((rs - (i - 1) * QROWS) * ROW_TOKENS, ROW_TOKENS))

    def qrows(j):
        return slice(j * ROW_TOKENS, (j + 1) * ROW_TOKENS)

    def lanes(t):
        return slice(t * PAIR_W, (t + 1) * PAIR_W)

    def scores(t):
        out = []
        for j in range(QROWS):
            q2 = q_ref[qrows(j), lanes(t)]
            zero = jnp.zeros_like(q2)
            qq = jnp.concatenate([jnp.where(lo_half, q2, zero),
                                  jnp.where(lo_half, zero, q2)], axis=0)
            kw = kbuf[pl.ds(offs[j], WIN_TOKENS), lanes(t)]
            s = lax.dot_general(qq, kw, (((1,), (1,)), ((), ())), preferred_element_type=F32)
            out.append(s + bias_ref[deltas[j], t])
        return out

    def softmax(s_list):
        out = []
        for s in s_list:
            m = jnp.max(s, axis=-1, keepdims=True)
            p = jnp.exp2(s - m)
            out.append((p.astype(BF16), jnp.sum(p, axis=-1, keepdims=True)))
        return out

    def weighted(t, p_list):
        for j, (p, l) in enumerate(p_list):
            vw = vbuf[pl.ds(offs[j], WIN_TOKENS), lanes(t)]
            o = jnp.dot(p, vw, preferred_element_type=F32) / l
            o2 = jnp.where(lo_half, o[:ROW_TOKENS], o[ROW_TOKENS:])
            o_ref[qrows(j), lanes(t)] = o2.astype(BF16)

    s_cur = scores(0)
    for t in range(HEAD_PAIRS):
        s_next = scores(t + 1) if t + 1 < HEAD_PAIRS else None
        p_cur = softmax(s_cur)
        weighted(t, p_cur)
        s_cur = s_next


def _natten(q, k, v, bias_tab):
    b, s, w = q.shape
    rows = s // GRID_W
    nblk = rows // QROWS
    blk = (None, QBLOCK, w)
    cur = pl.BlockSpec(blk, lambda bi, i: (bi, i, 0))
    prv = pl.BlockSpec(blk, lambda bi, i: (bi, jnp.maximum(i - 1, 0), 0))
    nxt = pl.BlockSpec(blk, lambda bi, i: (bi, jnp.minimum(i + 1, nblk - 1), 0))
    return pl.pallas_call(
        functools.partial(_natten_kernel, rows=rows),
        grid=(b, nblk),
        in_specs=[cur, prv, cur, nxt, prv, cur, nxt,
                  pl.BlockSpec(bias_tab.shape, lambda bi, i: (0, 0, 0, 0))],
        out_specs=cur,
        out_shape=jax.ShapeDtypeStruct((b, s, w), BF16),
        scratch_shapes=[pltpu.VMEM((3 * QBLOCK, w), BF16), pltpu.VMEM((3 * QBLOCK, w), BF16)],
        compiler_params=_cparams(("parallel", "parallel")),
        name="natten",
    )(q, k, k, k, v, v, v, bias_tab)


@functools.lru_cache(maxsize=None)
def _col_window_np():
    cols = np.arange(GRID_W)
    col_start = np.clip(cols - NA_COL_WIN // 2, 0, GRID_W - NA_COL_WIN)
    kc = cols[None, :]
    valid = (kc >= col_start[:, None]) & (kc < col_start[:, None] + NA_COL_WIN)
    dc = kc - cols[:, None] + (NA_COL_WIN - 1)
    onehot = (dc[None] == np.arange(2 * NA_COL_WIN - 1)[:, None, None]) & valid[None]
    return onehot.astype(np.float32), np.where(valid, 0.0, MASK_VALUE).astype(np.float32)


def _bias_table(rpb):
    onehot, maskadd = _col_window_np()
    tile = jnp.einsum('hrd,dcx->hrcx', rpb.astype(F32), jnp.asarray(onehot),
                      precision=lax.Precision.HIGHEST)
    tile = tile * LOG2E + jnp.asarray(maskadd)
    top = NA_ROW_WIN - 1
    tab = jnp.stack([tile[:, top - dl:top - dl + NA_ROW_WIN] for dl in range(NA_ROW_WIN)])
    tab = jnp.transpose(tab, (0, 1, 3, 2, 4))
    return tab.reshape(NA_ROW_WIN, HEAD_PAIRS, 2 * GRID_W, WIN_TOKENS)


def _merge_kernel(x_ref, y_ref, a_ref, gate_ref, wfa_ref, wnb_ref, wout_ref, o_ref):
    fa = jnp.dot(y_ref[...], wfa_ref[...], preferred_element_type=F32)
    nb = jnp.dot(a_ref[...], wnb_ref[...], preferred_element_type=F32)
    mixed = gate_ref[:, :D_MODEL].astype(F32) * fa + gate_ref[:, D_MODEL:].astype(F32) * nb
    o_ref[...] = x_ref[...] + jnp.dot(mixed.astype(BF16), wout_ref[...],
                                      preferred_element_type=F32)


def _merge(x2, y, att, gates, w_fa, w_nb, w_out, tm):
    n = x2.shape[0]
    tok = lambda w: pl.BlockSpec((tm, w), lambda i: (i, 0))
    full = lambda a: pl.BlockSpec(a.shape, lambda i: (0,) * a.ndim)
    return pl.pallas_call(
        _merge_kernel,
        grid=(n // tm,),
        in_specs=[tok(D_MODEL), tok(FNET_WIDTH), tok(NA_WIDTH), tok(2 * D_MODEL),
                  full(w_fa), full(w_nb), full(w_out)],
        out_specs=tok(D_MODEL),
        out_shape=jax.ShapeDtypeStruct((n, D_MODEL), F32),
        compiler_params=_cparams(("parallel",)),
        name="merge",
    )(x2, y, att, gates, w_fa, w_nb, w_out)


HALO = 8
FF_SUB = 256
FF_GROUP = 4


def _ffn_kernel(xp_ref, xc_ref, xn_ref, g2_ref, wup_ref, cw_ref, cb_ref, wdn_ref, gf_ref,
                o_ref, *, tm, tiles_per_seq, final_norm):
    t = pl.program_id(0) % tiles_per_seq
    xc = xc_ref[...]
    xp = jnp.where(t > 0, xp_ref[...], 0.0)
    xn = jnp.where(t < tiles_per_seq - 1, xn_ref[...], 0.0)
    xin = jnp.concatenate([xp, xc, xn], axis=0)
    h = _rms(xin, g2_ref[...]).astype(BF16)
    n_sub = D_FF // FF_SUB

    def up(c):
        return [jnp.dot(h, wup_ref[:, lo:lo + FF_SUB], preferred_element_type=F32)
                for lo in (c * FF_SUB, D_FF + c * FF_SUB)]

    def conv(u, lo):
        out = cb_ref[:, lo:lo + FF_SUB]
        for tap in range(3):
            out = out + u[HALO - 1 + tap:HALO - 1 + tap + tm] * cw_ref[tap:tap + 1, lo:lo + FF_SUB]
        return out

    acc = xc
    acts = []
    ups = up(0)
    for c in range(n_sub):
        nxt = up(c + 1) if c + 1 < n_sub else None
        val = conv(ups[0], c * FF_SUB)
        gate = conv(ups[1], D_FF + c * FF_SUB)
        acts.append((val * (gate * _sigmoid(gate))).astype(BF16))
        if len(acts) == FF_GROUP or c == n_sub - 1:
            lo = (c + 1 - len(acts)) * FF_SUB
            acc = acc + jnp.dot(jnp.concatenate(acts, axis=1), wdn_ref[lo:(c + 1) * FF_SUB, :],
                                preferred_element_type=F32)
            acts = []
        ups = nxt
    if final_norm:
        acc = _rms(acc, gf_ref[...])
    o_ref[...] = acc


def _ffn(x2, g2, w_up, conv_w, conv_b, w_down, gf, tm, seq, final_norm):
    n = x2.shape[0]
    hb = tm // HALO
    nhalo = n // HALO
    full = lambda a: pl.BlockSpec(a.shape, lambda i: (0,) * a.ndim)
    return pl.pallas_call(
        functools.partial(_ffn_kernel, tm=tm, tiles_per_seq=seq // tm, final_norm=final_norm),
        grid=(n // tm,),
        in_specs=[pl.BlockSpec((HALO, D_MODEL), lambda i: (jnp.maximum(i * hb - 1, 0), 0)),
                  pl.BlockSpec((tm, D_MODEL), lambda i: (i, 0)),
                  pl.BlockSpec((HALO, D_MODEL), lambda i: (jnp.minimum((i + 1) * hb, nhalo - 1), 0)),
                  full(g2), full(w_up), full(conv_w), full(conv_b), full(w_down), full(gf)],
        out_specs=pl.BlockSpec((tm, D_MODEL), lambda i: (i, 0)),
        out_shape=jax.ShapeDtypeStruct((n, D_MODEL), F32),
        compiler_params=_cparams(("parallel",)),
        name="ffn",
    )(x2, x2, x2, g2, w_up, conv_w, conv_b, w_down, gf)


def kernel(x, norm1_g, w_in, b_gate, rpb, w_fa, w_nb, w_out, norm2_g, w_up, conv_w, conv_b,
           w_down, final_g):
    b, s, d = x.shape
    depth = w_in.shape[0]
    r = FFT_RADIX
    assert s == r * r and d == D_MODEL and s % (GRID_W * QROWS) == 0
    wch, f1, gtab = _dft_tables()
    x2 = x.reshape(b * s, d)
    gf = final_g.reshape(1, d)
    for l in range(depth):
        zr, zi, q, k, v, gates = _inproj(
            x2, norm1_g[l].reshape(1, d), w_in[l].astype(BF16), b_gate[l], wch, tm=512)
        t = _fft1(zr.reshape(b, r, r * FNET_WIDTH), zi.reshape(b, r, r * FNET_WIDTH), f1, nb=4096)
        yt = _fft2(t.reshape(b, 2, r, r, FNET_WIDTH), gtab, kb=8)
        y = jnp.transpose(yt, (0, 2, 1, 3)).reshape(b * s, FNET_WIDTH)
        att = _natten(q.reshape(b, s, NA_WIDTH), k.reshape(b, s, NA_WIDTH),
                      v.reshape(b, s, NA_WIDTH), _bias_table(rpb[l]))
        x2 = _merge(x2, y, att.reshape(b * s, NA_WIDTH), gates,
                    w_fa[l].astype(BF16), w_nb[l].astype(BF16), w_out[l].astype(BF16), tm=512)
        x2 = _ffn(x2, norm2_g[l].reshape(1, d), w_up[l].astype(BF16), conv_w[l],
                  conv_b[l].reshape(1, 2 * D_FF), w_down[l].astype(BF16), gf,
                  tm=512, seq=s, final_norm=(l == depth - 1))
    return x2.reshape(b, s, d)
```

```python
import functools
import math

import jax
import jax.numpy as jnp
import numpy as np
from jax import lax
from jax.experimental import pallas as pl
from jax.experimental.pallas import tpu as pltpu

F32 = jnp.float32
BF16 = jnp.bfloat16

D_MODEL = 1024
GRID_W = 64
FNET_GROUPS = 4
FNET_GROUP_DIM = 128
FNET_WIDTH = FNET_GROUPS * FNET_GROUP_DIM
NA_HEADS = 8
NA_HEAD_DIM = 64
NA_WIDTH = NA_HEADS * NA_HEAD_DIM
NA_ROW_WIN = 8
NA_COL_WIN = 16
D_FF = 2816
RMS_EPS = 1e-6

FFT_RADIX = 128
LOG2E = 1.4426950408889634
MASK_VALUE = -1e30
HEAD_PAIRS = NA_HEADS // 2
PAIR_W = 2 * NA_HEAD_DIM
ROW_TOKENS = GRID_W
QROWS = 8
QBLOCK = QROWS * ROW_TOKENS
WIN_TOKENS = NA_ROW_WIN * ROW_TOKENS
KV_ROWS = QROWS + NA_ROW_WIN
KV_TOKENS = KV_ROWS * ROW_TOKENS

VMEM_LIMIT = 56 * 1024 * 1024


def _cparams(sem):
    return pltpu.CompilerParams(dimension_semantics=sem, vmem_limit_bytes=VMEM_LIMIT)


def _layer_spec(a, l):
    nd = a.ndim
    return pl.BlockSpec((None,) + a.shape[1:], lambda *_: (l,) + (0,) * (nd - 1))


def _full_spec(a):
    nd = a.ndim
    return pl.BlockSpec(a.shape, lambda *_: (0,) * nd)


def _rms(x, g):
    ms = jnp.mean(x * x, axis=-1, keepdims=True)
    return x * lax.rsqrt(ms + RMS_EPS) * g


def _sigmoid(x):
    return 1.0 / (1.0 + jnp.exp(-x))


def _inproj_kernel(x_ref, g1_ref, w_ref, bg_ref, wch_ref,
                   zr_ref, zi_ref, q_ref, k_ref, v_ref, gate_ref):
    h = _rms(x_ref[...], g1_ref[...]).astype(BF16)

    def proj(lo, hi):
        return jnp.dot(h, w_ref[:, lo:hi], preferred_element_type=F32)

    def fourier(u):
        u = u.astype(BF16)
        for g in range(FNET_GROUPS):
            sl = slice(g * FNET_GROUP_DIM, (g + 1) * FNET_GROUP_DIM)
            z = jnp.dot(u[:, sl], wch_ref[...], preferred_element_type=F32)
            zr_ref[:, sl] = z[:, :FNET_GROUP_DIM].astype(BF16)
            zi_ref[:, sl] = z[:, FNET_GROUP_DIM:].astype(BF16)

    def query(p):
        q_ref[...] = (p * (NA_HEAD_DIM ** -0.5 * LOG2E)).astype(BF16)

    def key(p):
        k_ref[...] = p.astype(BF16)

    def value(p):
        v_ref[...] = p.astype(BF16)

    def gate(j):
        def emit(p):
            gate_ref[:, j * D_MODEL:(j + 1) * D_MODEL] = _sigmoid(p + bg_ref[j:j + 1, :]).astype(BF16)
        return emit

    o = FNET_WIDTH + 3 * NA_WIDTH
    segments = [(0, FNET_WIDTH, fourier),
                (FNET_WIDTH, FNET_WIDTH + NA_WIDTH, query),
                (FNET_WIDTH + NA_WIDTH, FNET_WIDTH + 2 * NA_WIDTH, key),
                (FNET_WIDTH + 2 * NA_WIDTH, o, value),
                (o, o + D_MODEL, gate(0)),
                (o + D_MODEL, o + 2 * D_MODEL, gate(1))]
    cur = proj(*segments[0][:2])
    for s, (_, _, emit) in enumerate(segments):
        nxt = proj(*segments[s + 1][:2]) if s + 1 < len(segments) else None
        emit(cur)
        cur = nxt


def _inproj(x2, g1, w_in, b_gate, wch, l, tm):
    n = x2.shape[0]
    tok = lambda w: pl.BlockSpec((tm, w), lambda i: (i, 0))
    outs = [jax.ShapeDtypeStruct((n, FNET_WIDTH), BF16)] * 2 + \
           [jax.ShapeDtypeStruct((n, NA_WIDTH), BF16)] * 3 + \
           [jax.ShapeDtypeStruct((n, 2 * D_MODEL), BF16)]
    return pl.pallas_call(
        _inproj_kernel,
        grid=(n // tm,),
        in_specs=[tok(D_MODEL), _layer_spec(g1, l), _layer_spec(w_in, l), _layer_spec(b_gate, l),
                  _full_spec(wch)],
        out_specs=[tok(FNET_WIDTH)] * 2 + [tok(NA_WIDTH)] * 3 + [tok(2 * D_MODEL)],
        out_shape=outs,
        compiler_params=_cparams(("parallel",)),
        name="inproj",
    )(x2, g1, w_in, b_gate, wch)


def _fft1_kernel(zr_ref, zi_ref, f_ref, t_ref):
    rhs = jnp.concatenate([zr_ref[...], zi_ref[...]], axis=0)
    t_ref[...] = jnp.dot(f_ref[...], rhs, preferred_element_type=F32).astype(BF16)


def _fft1(zr2, zi2, f1, nb):
    b, r, n = zr2.shape
    zspec = pl.BlockSpec((None, r, nb), lambda bi, j: (bi, 0, j))
    return pl.pallas_call(
        _fft1_kernel,
        grid=(b, n // nb),
        in_specs=[zspec, zspec, _full_spec(f1)],
        out_specs=pl.BlockSpec((None, 2 * r, nb), lambda bi, j: (bi, 0, j)),
        out_shape=jax.ShapeDtypeStruct((b, 2 * r, n), BF16),
        compiler_params=_cparams(("parallel", "parallel")),
        name="fft1",
    )(zr2, zi2, f1)


def _fft2_kernel(t_ref, g_ref, y_ref, *, kb):
    for kk in range(kb):
        rhs = jnp.concatenate([t_ref[0, kk], t_ref[1, kk]], axis=0)
        y_ref[kk] = jnp.dot(g_ref[kk], rhs, preferred_element_type=F32).astype(BF16)


def _fft2(t5, gtab, kb):
    b, _, r, _, c = t5.shape
    return pl.pallas_call(
        functools.partial(_fft2_kernel, kb=kb),
        grid=(b, r // kb),
        in_specs=[pl.BlockSpec((None, 2, kb, r, c), lambda bi, j: (bi, 0, j, 0, 0)),
                  pl.BlockSpec((kb, r, 2 * r), lambda bi, j: (j, 0, 0))],
        out_specs=pl.BlockSpec((None, kb, r, c), lambda bi, j: (bi, j, 0, 0)),
        out_shape=jax.ShapeDtypeStruct((b, r, r, c), BF16),
        compiler_params=_cparams(("parallel", "parallel")),
        name="fft2",
    )(t5, gtab)


@functools.lru_cache(maxsize=None)
def _dft_tables_np():
    r = FFT_RADIX
    idx = np.arange(r, dtype=np.int64)
    scale = r ** -0.5
    ang = ((idx[:, None] * idx[None, :]) % r) * (2.0 * math.pi / r)
    c, s = np.cos(ang) * scale, np.sin(ang) * scale
    wch = np.concatenate([c, -s], axis=1)
    f1 = np.concatenate([np.concatenate([c, s], axis=1),
                         np.concatenate([-s, c], axis=1)], axis=0)
    k1 = idx[:, None, None]
    k2 = idx[None, :, None]
    bb = idx[None, None, :]
    th = ((bb * (r * k2 + k1)) % (r * r)) * (2.0 * math.pi / (r * r))
    gtab = np.concatenate([np.cos(th) * scale, np.sin(th) * scale], axis=2)
    return tuple(a.astype(np.float32) for a in (wch, f1, gtab))


def _dft_tables():
    return tuple(jnp.asarray(a).astype(BF16) for a in _dft_tables_np())


def _kv_start_row(i, rows):
    return jnp.clip(i * QROWS - NA_ROW_WIN // 2, 0, rows - KV_ROWS)


def _natten_kernel(q_ref, k_ref, v_ref, bias_ref, o_ref, *, rows):
    i = pl.program_id(1)
    lane = lax.broadcasted_iota(jnp.int32, (ROW_TOKENS, PAIR_W), 1)
    lo_half = lane < NA_HEAD_DIM
    start = _kv_start_row(i, rows)

    deltas, offs = [], []
    for j in range(QROWS):
        r = i * QROWS + j
        rs = jnp.clip(r - NA_ROW_WIN // 2, 0, rows - NA_ROW_WIN)
        deltas.append(r - rs)
        offs.append(pl.multiple_of((rs - start) * ROW_TOKENS, ROW_TOKENS))

    def qrows(j):
        return slice(j * ROW_TOKENS, (j + 1) * ROW_TOKENS)

    def lanes(t):
        return slice(t * PAIR_W, (t + 1) * PAIR_W)

    def scores(t):
        out = []
        for j in range(QROWS):
            q2 = q_ref[qrows(j), lanes(t)]
            zero = jnp.zeros_like(q2)
            qq = jnp.concatenate([jnp.where(lo_half, q2, zero),
                                  jnp.where(lo_half, zero, q2)], axis=0)
            kw = k_ref[pl.ds(offs[j], WIN_TOKENS), lanes(t)]
            s = lax.dot_general(qq, kw, (((1,), (1,)), ((), ())), preferred_element_type=F32)
            out.append(s + bias_ref[deltas[j], t])
        return out

    def softmax(s_list):
        out = []
        for s in s_list:
            m = jnp.max(s, axis=-1, keepdims=True)
            p = jnp.exp2(s - m)
            out.append((p.astype(BF16), jnp.sum(p, axis=-1, keepdims=True)))
        return out

    def weighted(t, p_list):
        for j, (p, l) in enumerate(p_list):
            vw = v_ref[pl.ds(offs[j], WIN_TOKENS), lanes(t)]
            o = jnp.dot(p, vw, preferred_element_type=F32) / l
            o2 = jnp.where(lo_half, o[:ROW_TOKENS], o[ROW_TOKENS:])
            o_ref[qrows(j), lanes(t)] = o2.astype(BF16)

    s_cur = scores(0)
    for t in range(HEAD_PAIRS):
        s_next = scores(t + 1) if t + 1 < HEAD_PAIRS else None
        p_cur = softmax(s_cur)
        weighted(t, p_cur)
        s_cur = s_next


def _natten(q, k, v, bias_tab, l):
    b, s, w = q.shape
    rows = s // GRID_W
    qspec = pl.BlockSpec((None, QBLOCK, w), lambda bi, i: (bi, i, 0))
    kvspec = pl.BlockSpec((None, pl.Element(KV_TOKENS), pl.Element(w)),
                          lambda bi, i: (bi, _kv_start_row(i, rows) * ROW_TOKENS, 0))
    return pl.pallas_call(
        functools.partial(_natten_kernel, rows=rows),
        grid=(b, rows // QROWS),
        in_specs=[qspec, kvspec, kvspec, _layer_spec(bias_tab, l)],
        out_specs=qspec,
        out_shape=jax.ShapeDtypeStruct((b, s, w), BF16),
        compiler_params=_cparams(("parallel", "parallel")),
        name="natten",
    )(q, k, v, bias_tab)


@functools.lru_cache(maxsize=None)
def _bias_maps_np():
    cols = np.arange(GRID_W)
    col_start = np.clip(cols - NA_COL_WIN // 2, 0, GRID_W - NA_COL_WIN)
    kc = cols[None, :]
    valid = (kc >= col_start[:, None]) & (kc < col_start[:, None] + NA_COL_WIN)
    dc = kc - cols[:, None] + (NA_COL_WIN - 1)
    col_map = (dc[None] == np.arange(2 * NA_COL_WIN - 1)[:, None, None]) & valid[None]
    delta = np.arange(NA_ROW_WIN)[:, None, None]
    wi = np.arange(NA_ROW_WIN)[None, :, None]
    row_map = (np.arange(2 * NA_ROW_WIN - 1)[None, None, :] == wi + (NA_ROW_WIN - 1) - delta)
    maskadd = np.where(valid, 0.0, MASK_VALUE)
    return col_map.astype(np.float32), row_map.astype(np.float32), maskadd.astype(np.float32)


def _bias_table(rpb):
    col_map, row_map, maskadd = _bias_maps_np()
    hi = lax.Precision.HIGHEST
    t = jnp.einsum('lhrd,dcx->lhrcx', rpb.astype(F32), jnp.asarray(col_map), precision=hi)
    t = jnp.einsum('lhrcx,dir->ldhcix', t, jnp.asarray(row_map), precision=hi)
    t = t * LOG2E + jnp.asarray(maskadd)[:, None, :]
    return t.reshape(rpb.shape[0], NA_ROW_WIN, HEAD_PAIRS, 2 * GRID_W, WIN_TOKENS)


def _merge_kernel(x_ref, y_ref, a_ref, gate_ref, wfa_ref, wnb_ref, wout_ref, o_ref):
    fa = jnp.dot(y_ref[...], wfa_ref[...], preferred_element_type=F32)
    nb = jnp.dot(a_ref[...], wnb_ref[...], preferred_element_type=F32)
    mixed = gate_ref[:, :D_MODEL].astype(F32) * fa + gate_ref[:, D_MODEL:].astype(F32) * nb
    o_ref[...] = x_ref[...] + jnp.dot(mixed.astype(BF16), wout_ref[...],
                                      preferred_element_type=F32)


def _merge(x2, y, att, gates, w_fa, w_nb, w_out, l, tm):
    n = x2.shape[0]
    tok = lambda w: pl.BlockSpec((tm, w), lambda i: (i, 0))
    return pl.pallas_call(
        _merge_kernel,
        grid=(n // tm,),
        in_specs=[tok(D_MODEL), tok(FNET_WIDTH), tok(NA_WIDTH), tok(2 * D_MODEL),
                  _layer_spec(w_fa, l), _layer_spec(w_nb, l), _layer_spec(w_out, l)],
        out_specs=tok(D_MODEL),
        out_shape=jax.ShapeDtypeStruct((n, D_MODEL), F32),
        compiler_params=_cparams(("parallel",)),
        name="merge",
    )(x2, y, att, gates, w_fa, w_nb, w_out)


HALO = 8
FF_SUB = 256
FF_GROUP = 4


def _ffn_kernel(xp_ref, xc_ref, xn_ref, g2_ref, wup_ref, cw_ref, cb_ref, wdn_ref, gf_ref,
                o_ref, *, tm, tiles_per_seq, final_norm):
    t = pl.program_id(0) % tiles_per_seq
    xc = xc_ref[...]
    xp = jnp.where(t > 0, xp_ref[...], 0.0)
    xn = jnp.where(t < tiles_per_seq - 1, xn_ref[...], 0.0)
    xin = jnp.concatenate([xp, xc, xn], axis=0)
    h = _rms(xin, g2_ref[...]).astype(BF16)
    n_sub = D_FF // FF_SUB

    def up(c):
        return [jnp.dot(h, wup_ref[:, lo:lo + FF_SUB], preferred_element_type=F32)
                for lo in (c * FF_SUB, D_FF + c * FF_SUB)]

    def conv(u, lo):
        out = cb_ref[:, lo:lo + FF_SUB]
        for tap in range(3):
            out = out + u[HALO - 1 + tap:HALO - 1 + tap + tm] * cw_ref[tap:tap + 1, lo:lo + FF_SUB]
        return out

    acc = xc
    acts = []
    ups = up(0)
    for c in range(n_sub):
        nxt = up(c + 1) if c + 1 < n_sub else None
        val = conv(ups[0], c * FF_SUB)
        gate = conv(ups[1], D_FF + c * FF_SUB)
        acts.append((val * (gate * _sigmoid(gate))).astype(BF16))
        if len(acts) == FF_GROUP or c == n_sub - 1:
            lo = (c + 1 - len(acts)) * FF_SUB
            acc = acc + jnp.dot(jnp.concatenate(acts, axis=1), wdn_ref[lo:(c + 1) * FF_SUB, :],
                                preferred_element_type=F32)
            acts = []
        ups = nxt
    if final_norm:
        acc = _rms(acc, gf_ref[...])
    o_ref[...] = acc


def _ffn(x2, g2, w_up, conv_w, conv_b, w_down, gf, l, tm, seq, final_norm):
    n = x2.shape[0]
    hb = tm // HALO
    nhalo = n // HALO
    return pl.pallas_call(
        functools.partial(_ffn_kernel, tm=tm, tiles_per_seq=seq // tm, final_norm=final_norm),
        grid=(n // tm,),
        in_specs=[pl.BlockSpec((HALO, D_MODEL), lambda i: (jnp.maximum(i * hb - 1, 0), 0)),
                  pl.BlockSpec((tm, D_MODEL), lambda i: (i, 0)),
                  pl.BlockSpec((HALO, D_MODEL), lambda i: (jnp.minimum((i + 1) * hb, nhalo - 1), 0)),
                  _layer_spec(g2, l), _layer_spec(w_up, l), _layer_spec(conv_w, l),
                  _layer_spec(conv_b, l), _layer_spec(w_down, l), _full_spec(gf)],
        out_specs=pl.BlockSpec((tm, D_MODEL), lambda i: (i, 0)),
        out_shape=jax.ShapeDtypeStruct((n, D_MODEL), F32),
        compiler_params=_cparams(("parallel",)),
        name="ffn",
    )(x2, x2, x2, g2, w_up, conv_w, conv_b, w_down, gf)


def kernel(x, norm1_g, w_in, b_gate, rpb, w_fa, w_nb, w_out, norm2_g, w_up, conv_w, conv_b,
           w_down, final_g):
    b, s, d = x.shape
    depth = w_in.shape[0]
    r = FFT_RADIX
    assert s == r * r and d == D_MODEL and s % (GRID_W * QROWS) == 0
    wch, f1, gtab = _dft_tables()
    bias_tab = _bias_table(rpb)
    g1, g2 = norm1_g.reshape(depth, 1, d), norm2_g.reshape(depth, 1, d)
    cb = conv_b.reshape(depth, 1, 2 * D_FF)
    gf = final_g.reshape(1, d)
    w_in, w_fa, w_nb, w_out, w_up, w_down = (
        w.astype(BF16) for w in (w_in, w_fa, w_nb, w_out, w_up, w_down))
    x2 = x.reshape(b * s, d)
    for l in range(depth):
        zr, zi, q, k, v, gates = _inproj(x2, g1, w_in, b_gate, wch, l, tm=512)
        t = _fft1(zr.reshape(b, r, r * FNET_WIDTH), zi.reshape(b, r, r * FNET_WIDTH), f1, nb=4096)
        yt = _fft2(t.reshape(b, 2, r, r, FNET_WIDTH), gtab, kb=8)
        y = jnp.transpose(yt, (0, 2, 1, 3)).reshape(b * s, FNET_WIDTH)
        att = _natten(q.reshape(b, s, NA_WIDTH), k.reshape(b, s, NA_WIDTH),
                      v.reshape(b, s, NA_WIDTH), bias_tab, l)
        x2 = _merge(x2, y, att.reshape(b * s, NA_WIDTH), gates, w_fa, w_nb, w_out, l, tm=512)
        x2 = _ffn(x2, g2, w_up, conv_w, cb, w_down, gf, l,
                  tm=512, seq=s, final_norm=(l == depth - 1))
    return x2.reshape(b, s, d)
```

```python
import functools
import math

import jax
import jax.numpy as jnp
import numpy as np
from jax import lax
from jax.experimental import pallas as pl
from jax.experimental.pallas import tpu as pltpu

F32 = jnp.float32
BF16 = jnp.bfloat16

D_MODEL = 1024
GRID_W = 64
FNET_GROUPS = 4
FNET_GROUP_DIM = 128
FNET_WIDTH = FNET_GROUPS * FNET_GROUP_DIM
NA_HEADS = 8
NA_HEAD_DIM = 64
NA_WIDTH = NA_HEADS * NA_HEAD_DIM
NA_ROW_WIN = 8
NA_COL_WIN = 16
D_FF = 2816
RMS_EPS = 1e-6

FFT_RADIX = 128
LOG2E = 1.4426950408889634
MASK_VALUE = -1e30
HEAD_PAIRS = NA_HEADS // 2
PAIR_W = 2 * NA_HEAD_DIM
ROW_TOKENS = GRID_W
QROWS = 8
QBLOCK = QROWS * ROW_TOKENS
WIN_TOKENS = NA_ROW_WIN * ROW_TOKENS
KV_ROWS = QROWS + NA_ROW_WIN
KV_TOKENS = KV_ROWS * ROW_TOKENS

VMEM_LIMIT = 56 * 1024 * 1024


def _cparams(sem):
    return pltpu.CompilerParams(dimension_semantics=sem, vmem_limit_bytes=VMEM_LIMIT)


def _layer_spec(a, l):
    nd = a.ndim
    return pl.BlockSpec((None,) + a.shape[1:], lambda *_: (l,) + (0,) * (nd - 1))


def _full_spec(a):
    nd = a.ndim
    return pl.BlockSpec(a.shape, lambda *_: (0,) * nd)


def _rms(x, g):
    ms = jnp.mean(x * x, axis=-1, keepdims=True)
    return x * lax.rsqrt(ms + RMS_EPS) * g


def _sigmoid(x):
    return 1.0 / (1.0 + jnp.exp(-x))


def _inproj_kernel(x_ref, g1_ref, w_ref, bg_ref, wch_ref,
                   zr_ref, zi_ref, q_ref, k_ref, v_ref, gate_ref):
    h = _rms(x_ref[...], g1_ref[...]).astype(BF16)

    def proj(lo, hi):
        return jnp.dot(h, w_ref[:, lo:hi], preferred_element_type=F32)

    def fourier(u):
        u = u.astype(BF16)
        for g in range(FNET_GROUPS):
            sl = slice(g * FNET_GROUP_DIM, (g + 1) * FNET_GROUP_DIM)
            z = jnp.dot(u[:, sl], wch_ref[...], preferred_element_type=F32)
            zr_ref[:, sl] = z[:, :FNET_GROUP_DIM].astype(BF16)
            zi_ref[:, sl] = z[:, FNET_GROUP_DIM:].astype(BF16)

    def query(p):
        q_ref[...] = (p * (NA_HEAD_DIM ** -0.5 * LOG2E)).astype(BF16)

    def key(p):
        k_ref[...] = p.astype(BF16)

    def value(p):
        v_ref[...] = p.astype(BF16)

    def gate(j):
        def emit(p):
            gate_ref[:, j * D_MODEL:(j + 1) * D_MODEL] = _sigmoid(p + bg_ref[j:j + 1, :]).astype(BF16)
        return emit

    o = FNET_WIDTH + 3 * NA_WIDTH
    segments = [(0, FNET_WIDTH, fourier),
                (FNET_WIDTH, FNET_WIDTH + NA_WIDTH, query),
                (FNET_WIDTH + NA_WIDTH, FNET_WIDTH + 2 * NA_WIDTH, key),
                (FNET_WIDTH + 2 * NA_WIDTH, o, value),
                (o, o + D_MODEL, gate(0)),
                (o + D_MODEL, o + 2 * D_MODEL, gate(1))]
    cur = proj(*segments[0][:2])
    for s, (_, _, emit) in enumerate(segments):
        nxt = proj(*segments[s + 1][:2]) if s + 1 < len(segments) else None
        emit(cur)
        cur = nxt


def _inproj(x2, g1, w_in, b_gate, wch, l, tm):
    n = x2.shape[0]
    tok = lambda w: pl.BlockSpec((tm, w), lambda i: (i, 0))
    outs = [jax.ShapeDtypeStruct((n, FNET_WIDTH), BF16)] * 2 + \
           [jax.ShapeDtypeStruct((n, NA_WIDTH), BF16)] * 3 + \
           [jax.ShapeDtypeStruct((n, 2 * D_MODEL), BF16)]
    return pl.pallas_call(
        _inproj_kernel,
        grid=(n // tm,),
        in_specs=[tok(D_MODEL), _layer_spec(g1, l), _layer_spec(w_in, l), _layer_spec(b_gate, l),
                  _full_spec(wch)],
        out_specs=[tok(FNET_WIDTH)] * 2 + [tok(NA_WIDTH)] * 3 + [tok(2 * D_MODEL)],
        out_shape=outs,
        compiler_params=_cparams(("parallel",)),
        name="inproj",
    )(x2, g1, w_in, b_gate, wch)


def _fft1_kernel(zr_ref, zi_ref, f_ref, t_ref):
    rhs = jnp.concatenate([zr_ref[...], zi_ref[...]], axis=0)
    t_ref[...] = jnp.dot(f_ref[...], rhs, preferred_element_type=F32).astype(BF16)


def _fft1(zr2, zi2, f1, nb):
    b, r, n = zr2.shape
    zspec = pl.BlockSpec((None, r, nb), lambda bi, j: (bi, 0, j))
    return pl.pallas_call(
        _fft1_kernel,
        grid=(b, n // nb),
        in_specs=[zspec, zspec, _full_spec(f1)],
        out_specs=pl.BlockSpec((None, 2 * r, nb), lambda bi, j: (bi, 0, j)),
        out_shape=jax.ShapeDtypeStruct((b, 2 * r, n), BF16),
        compiler_params=_cparams(("parallel", "parallel")),
        name="fft1",
    )(zr2, zi2, f1)


def _fft2_kernel(t_ref, g_ref, anchor_ref, y_ref, *, kb):
    del anchor_ref
    for kk in range(kb):
        rhs = jnp.concatenate([t_ref[0, kk], t_ref[1, kk]], axis=0)
        y_ref[kk] = jnp.dot(g_ref[kk], rhs, preferred_element_type=F32).astype(BF16)


def _fft2(t5, gtab, anchor, kb):
    b, _, r, _, c = t5.shape
    return pl.pallas_call(
        functools.partial(_fft2_kernel, kb=kb),
        grid=(b, r // kb),
        in_specs=[pl.BlockSpec((None, 2, kb, r, c), lambda bi, j: (bi, 0, j, 0, 0)),
                  pl.BlockSpec((kb, r, 2 * r), lambda bi, j: (j, 0, 0)),
                  pl.BlockSpec(memory_space=pl.ANY)],
        out_specs=pl.BlockSpec((None, kb, r, c), lambda bi, j: (bi, j, 0, 0)),
        out_shape=jax.ShapeDtypeStruct((b, r, r, c), BF16),
        compiler_params=_cparams(("parallel", "parallel")),
        name="fft2",
    )(t5, gtab, anchor)


@functools.lru_cache(maxsize=None)
def _dft_tables_np():
    r = FFT_RADIX
    idx = np.arange(r, dtype=np.int64)
    scale = r ** -0.5
    ang = ((idx[:, None] * idx[None, :]) % r) * (2.0 * math.pi / r)
    c, s = np.cos(ang) * scale, np.sin(ang) * scale
    wch = np.concatenate([c, -s], axis=1)
    f1 = np.concatenate([np.concatenate([c, s], axis=1),
                         np.concatenate([-s, c], axis=1)], axis=0)
    k1 = idx[:, None, None]
    k2 = idx[None, :, None]
    bb = idx[None, None, :]
    th = ((bb * (r * k2 + k1)) % (r * r)) * (2.0 * math.pi / (r * r))
    gtab = np.concatenate([np.cos(th) * scale, np.sin(th) * scale], axis=2)
    return tuple(a.astype(np.float32) for a in (wch, f1, gtab))


def _dft_tables():
    return tuple(jnp.asarray(a).astype(BF16) for a in _dft_tables_np())


def _kv_start_row(i, rows):
    return jnp.clip(i * QROWS - NA_ROW_WIN // 2, 0, rows - KV_ROWS)


CHAIN_LAG = 2


def _natten_kernel(q_ref, k_ref, v_ref, bias_ref, anchor_ref, o_ref, *, rows):
    del anchor_ref
    i = pl.program_id(0)
    lane = lax.broadcasted_iota(jnp.int32, (ROW_TOKENS, PAIR_W), 1)
    lo_half = lane < NA_HEAD_DIM
    start = _kv_start_row(i, rows)

    deltas, offs = [], []
    for j in range(QROWS):
        r = i * QROWS + j
        rs = jnp.clip(r - NA_ROW_WIN // 2, 0, rows - NA_ROW_WIN)
        deltas.append(r - rs)
        offs.append(pl.multiple_of((rs - start) * ROW_TOKENS, ROW_TOKENS))

    def qrows(j):
        return slice(j * ROW_TOKENS, (j + 1) * ROW_TOKENS)

    def lanes(t):
        return slice(t * PAIR_W, (t + 1) * PAIR_W)

    def scores(t, j):
        q2 = q_ref[qrows(j), lanes(t)]
        zero = jnp.zeros_like(q2)
        qq = jnp.concatenate([jnp.where(lo_half, q2, zero),
                              jnp.where(lo_half, zero, q2)], axis=0)
        kw = k_ref[pl.ds(offs[j], WIN_TOKENS), lanes(t)]
        s = lax.dot_general(qq, kw, (((1,), (1,)), ((), ())), preferred_element_type=F32)
        return s + bias_ref[deltas[j], t]

    def softmax(s):
        m = jnp.max(s, axis=-1, keepdims=True)
        p = jnp.exp2(s - m)
        return p.astype(BF16), jnp.sum(p, axis=-1, keepdims=True)

    def weighted(t, j, p, l):
        vw = v_ref[pl.ds(offs[j], WIN_TOKENS), lanes(t)]
        o = jnp.dot(p, vw, preferred_element_type=F32) / l
        o2 = jnp.where(lo_half, o[:ROW_TOKENS], o[ROW_TOKENS:])
        o_ref[qrows(j), lanes(t)] = o2.astype(BF16)

    chains = [(t, j) for t in range(HEAD_PAIRS) for j in range(QROWS)]
    sc, sm = {}, {}
    for step in range(len(chains) + 2 * CHAIN_LAG):
        if step < len(chains):
            sc[step] = scores(*chains[step])
        n = step - CHAIN_LAG
        if 0 <= n < len(chains):
            sm[n] = softmax(sc.pop(n))
        n = step - 2 * CHAIN_LAG
        if 0 <= n < len(chains):
            weighted(*chains[n], *sm.pop(n))


def _natten(q, k, v, bias_tab, l, bi, anchor):
    _, s, w = q.shape
    rows = s // GRID_W
    qspec = pl.BlockSpec((None, QBLOCK, w), lambda i: (bi, i, 0))
    kvspec = pl.BlockSpec((None, pl.Element(KV_TOKENS), pl.Element(w)),
                          lambda i: (bi, _kv_start_row(i, rows) * ROW_TOKENS, 0))
    return pl.pallas_call(
        functools.partial(_natten_kernel, rows=rows),
        grid=(rows // QROWS,),
        in_specs=[qspec, kvspec, kvspec, _layer_spec(bias_tab, l), pl.BlockSpec(memory_space=pl.ANY)],
        out_specs=pl.BlockSpec((QBLOCK, w), lambda i: (i, 0)),
        out_shape=jax.ShapeDtypeStruct((s, w), BF16),
        compiler_params=_cparams(("parallel",)),
        name="natten",
    )(q, k, v, bias_tab, anchor)


@functools.lru_cache(maxsize=None)
def _bias_maps_np():
    cols = np.arange(GRID_W)
    col_start = np.clip(cols - NA_COL_WIN // 2, 0, GRID_W - NA_COL_WIN)
    kc = cols[None, :]
    valid = (kc >= col_start[:, None]) & (kc < col_start[:, None] + NA_COL_WIN)
    dc = kc - cols[:, None] + (NA_COL_WIN - 1)
    col_map = (dc[None] == np.arange(2 * NA_COL_WIN - 1)[:, None, None]) & valid[None]
    delta = np.arange(NA_ROW_WIN)[:, None, None]
    wi = np.arange(NA_ROW_WIN)[None, :, None]
    row_map = (np.arange(2 * NA_ROW_WIN - 1)[None, None, :] == wi + (NA_ROW_WIN - 1) - delta)
    maskadd = np.where(valid, 0.0, MASK_VALUE)
    return col_map.astype(np.float32), row_map.astype(np.float32), maskadd.astype(np.float32)


def _bias_table(rpb):
    col_map, row_map, maskadd = _bias_maps_np()
    hi = lax.Precision.HIGHEST
    t = jnp.einsum('lhrd,dcx->lhrcx', rpb.astype(F32), jnp.asarray(col_map), precision=hi)
    t = jnp.einsum('lhrcx,dir->ldhcix', t, jnp.asarray(row_map), precision=hi)
    t = t * LOG2E + jnp.asarray(maskadd)[:, None, :]
    return t.reshape(rpb.shape[0], NA_ROW_WIN, HEAD_PAIRS, 2 * GRID_W, WIN_TOKENS)


def _merge_kernel(x_ref, y_ref, gate_ref, wfa_ref, wnb_ref, wout_ref, *rest, tiles_per_seq):
    att_refs, o_ref = rest[:-1], rest[-1]
    bi = pl.program_id(0) // tiles_per_seq
    att = att_refs[0][...]
    for b in range(1, len(att_refs)):
        att = jnp.where(bi == b, att_refs[b][...], att)
    fa = jnp.dot(y_ref[...], wfa_ref[...], preferred_element_type=F32)
    nb = jnp.dot(att, wnb_ref[...], preferred_element_type=F32)
    mixed = gate_ref[:, :D_MODEL].astype(F32) * fa + gate_ref[:, D_MODEL:].astype(F32) * nb
    o_ref[...] = x_ref[...] + jnp.dot(mixed.astype(BF16), wout_ref[...],
                                      preferred_element_type=F32)


def _merge(x2, y, atts, gates, w_fa, w_nb, w_out, l, tm):
    n = x2.shape[0]
    tps = atts[0].shape[0] // tm
    tok = lambda w: pl.BlockSpec((tm, w), lambda i: (i, 0))
    att_spec = lambda b: pl.BlockSpec((tm, NA_WIDTH), lambda i: (jnp.clip(i - b * tps, 0, tps - 1), 0))
    return pl.pallas_call(
        functools.partial(_merge_kernel, tiles_per_seq=tps),
        grid=(n // tm,),
        in_specs=[tok(D_MODEL), tok(FNET_WIDTH), tok(2 * D_MODEL),
                  _layer_spec(w_fa, l), _layer_spec(w_nb, l), _layer_spec(w_out, l)] +
                 [att_spec(b) for b in range(len(atts))],
        out_specs=tok(D_MODEL),
        out_shape=jax.ShapeDtypeStruct((n, D_MODEL), F32),
        compiler_params=_cparams(("parallel",)),
        name="merge",
    )(x2, y, gates, w_fa, w_nb, w_out, *atts)


HALO = 8
FF_SUB = 768
FF_GROUP = 2


def _ffn_kernel(xp_ref, xc_ref, xn_ref, g2_ref, wup_ref, cw_ref, cb_ref, wdn_ref, gf_ref,
                o_ref, *, tm, tiles_per_seq, final_norm):
    t = pl.program_id(0) % tiles_per_seq
    xc = xc_ref[...]
    xp = jnp.where(t > 0, xp_ref[...], 0.0)
    xn = jnp.where(t < tiles_per_seq - 1, xn_ref[...], 0.0)
    xin = jnp.concatenate([xp, xc, xn], axis=0)
    h = _rms(xin, g2_ref[...]).astype(BF16)
    bounds = list(range(0, D_FF, FF_SUB)) + [D_FF]
    subs = list(zip(bounds[:-1], bounds[1:]))

    def up(c):
        lo, hi = subs[c]
        return [jnp.dot(h, wup_ref[:, o + lo:o + hi], preferred_element_type=F32) for o in (0, D_FF)]

    def conv(u, lo, hi):
        shifted = (pltpu.roll(u, 1, 0), u, pltpu.roll(u, tm + 2 * HALO - 1, 0))
        out = cb_ref[:, lo:hi]
        for tap, arr in enumerate(shifted):
            out = out + arr[HALO:HALO + tm] * cw_ref[tap:tap + 1, lo:hi]
        return out

    acc = xc
    acts = []
    ups = up(0)
    for c, (lo, hi) in enumerate(subs):
        nxt = up(c + 1) if c + 1 < len(subs) else None
        val = conv(ups[0], lo, hi)
        gate = conv(ups[1], D_FF + lo, D_FF + hi)
        acts.append((val * (gate * _sigmoid(gate))).astype(BF16))
        if len(acts) == FF_GROUP or c == len(subs) - 1:
            glo = subs[c + 1 - len(acts)][0]
            acc = acc + jnp.dot(jnp.concatenate(acts, axis=1), wdn_ref[glo:hi, :],
                                preferred_element_type=F32)
            acts = []
        ups = nxt
    if final_norm:
        acc = _rms(acc, gf_ref[...])
    o_ref[...] = acc


def _ffn(x2, g2, w_up, conv_w, conv_b, w_down, gf, l, tm, seq, final_norm):
    n = x2.shape[0]
    hb = tm // HALO
    nhalo = n // HALO
    return pl.pallas_call(
        functools.partial(_ffn_kernel, tm=tm, tiles_per_seq=seq // tm, final_norm=final_norm),
        grid=(n // tm,),
        in_specs=[pl.BlockSpec((HALO, D_MODEL), lambda i: (jnp.maximum(i * hb - 1, 0), 0)),
                  pl.BlockSpec((tm, D_MODEL), lambda i: (i, 0)),
                  pl.BlockSpec((HALO, D_MODEL), lambda i: (jnp.minimum((i + 1) * hb, nhalo - 1), 0)),
                  _layer_spec(g2, l), _layer_spec(w_up, l), _layer_spec(conv_w, l),
                  _layer_spec(conv_b, l), _layer_spec(w_down, l), _full_spec(gf)],
        out_specs=pl.BlockSpec((tm, D_MODEL), lambda i: (i, 0)),
        out_shape=jax.ShapeDtypeStruct((n, D_MODEL), F32),
        compiler_params=_cparams(("parallel",)),
        name="ffn",
    )(x2, x2, x2, g2, w_up, conv_w, conv_b, w_down, gf)


def kernel(x, norm1_g, w_in, b_gate, rpb, w_fa, w_nb, w_out, norm2_g, w_up, conv_w, conv_b,
           w_down, final_g):
    b, s, d = x.shape
    depth = w_in.shape[0]
    r = FFT_RADIX
    assert s == r * r and d == D_MODEL and s % (GRID_W * QROWS) == 0
    wch, f1, gtab = _dft_tables()
    bias_tab = _bias_table(rpb)
    g1, g2 = norm1_g.reshape(depth, 1, d), norm2_g.reshape(depth, 1, d)
    cb = conv_b.reshape(depth, 1, 2 * D_FF)
    gf = final_g.reshape(1, d)
    w_in, w_fa, w_nb, w_out, w_up, w_down = (
        w.astype(BF16) for w in (w_in, w_fa, w_nb, w_out, w_up, w_down))
    x2 = x.reshape(b * s, d)
    for l in range(depth):
        zr, zi, q, k, v, gates = _inproj(x2, g1, w_in, b_gate, wch, l, tm=512)
        q, k, v = (a.reshape(b, s, NA_WIDTH) for a in (q, k, v))
        t = _fft1(zr.reshape(b, r, r * FNET_WIDTH), zi.reshape(b, r, r * FNET_WIDTH), f1, nb=4096)
        atts = [_natten(q, k, v, bias_tab, l, bi, t) for bi in range(b - 1)]
        yt = _fft2(t.reshape(b, 2, r, r, FNET_WIDTH), gtab, atts[-1] if atts else t, kb=8)
        atts.append(_natten(q, k, v, bias_tab, l, b - 1, yt))
        y = jnp.transpose(yt, (0, 2, 1, 3)).reshape(b * s, FNET_WIDTH)
        x2 = _merge(x2, y, atts, gates, w_fa, w_nb, w_out, l, tm=512)
        x2 = _ffn(x2, g2, w_up, conv_w, cb, w_down, gf, l,
                  tm=512, seq=s, final_norm=(l == depth - 1))
    return x2.reshape(b, s, d)
```

```python
import functools
import math

import jax
import jax.numpy as jnp
import numpy as np
from jax import lax
from jax.experimental import pallas as pl
from jax.experimental.pallas import tpu as pltpu

F32 = jnp.float32
BF16 = jnp.bfloat16

D_MODEL = 1024
GRID_W = 64
FNET_GROUPS = 4
FNET_GROUP_DIM = 128
FNET_WIDTH = FNET_GROUPS * FNET_GROUP_DIM
NA_HEADS = 8
NA_HEAD_DIM = 64
NA_WIDTH = NA_HEADS * NA_HEAD_DIM
NA_ROW_WIN = 8
NA_COL_WIN = 16
D_FF = 2816
RMS_EPS = 1e-6

FFT_RADIX = 128
LOG2E = 1.4426950408889634
MASK_VALUE = -1e30
HEAD_PAIRS = NA_HEADS // 2
PAIR_W = 2 * NA_HEAD_DIM
ROW_TOKENS = GRID_W
QROWS = 8
QBLOCK = QROWS * ROW_TOKENS
WIN_TOKENS = NA_ROW_WIN * ROW_TOKENS
KV_ROWS = QROWS + NA_ROW_WIN
KV_TOKENS = KV_ROWS * ROW_TOKENS

VMEM_LIMIT = 56 * 1024 * 1024


def _cparams(sem):
    return pltpu.CompilerParams(dimension_semantics=sem, vmem_limit_bytes=VMEM_LIMIT)


def _layer_spec(a, l):
    nd = a.ndim
    return pl.BlockSpec((None,) + a.shape[1:], lambda *_: (l,) + (0,) * (nd - 1))


def _full_spec(a):
    nd = a.ndim
    return pl.BlockSpec(a.shape, lambda *_: (0,) * nd)


def _rms(x, g):
    ms = jnp.mean(x * x, axis=-1, keepdims=True)
    return x * lax.rsqrt(ms + RMS_EPS) * g


def _sigmoid(x):
    return 1.0 / (1.0 + jnp.exp(-x))


def _inproj_kernel(x_ref, g1_ref, w_ref, bg_ref, q_ref, k_ref, v_ref, gate_ref):
    h = _rms(x_ref[...], g1_ref[...]).astype(BF16)

    def proj(lo, hi):
        return jnp.dot(h, w_ref[:, lo:hi], preferred_element_type=F32)

    def query(p):
        q_ref[...] = (p * (NA_HEAD_DIM ** -0.5 * LOG2E)).astype(BF16)

    def key(p):
        k_ref[...] = p.astype(BF16)

    def value(p):
        v_ref[...] = p.astype(BF16)

    def gate(j):
        def emit(p):
            gate_ref[:, j * D_MODEL:(j + 1) * D_MODEL] = _sigmoid(p + bg_ref[j:j + 1, :]).astype(BF16)
        return emit

    o = FNET_WIDTH + 3 * NA_WIDTH
    segments = [(FNET_WIDTH, FNET_WIDTH + NA_WIDTH, query),
                (FNET_WIDTH + NA_WIDTH, FNET_WIDTH + 2 * NA_WIDTH, key),
                (FNET_WIDTH + 2 * NA_WIDTH, o, value),
                (o, o + D_MODEL, gate(0)),
                (o + D_MODEL, o + 2 * D_MODEL, gate(1))]
    cur = proj(*segments[0][:2])
    for s, (_, _, emit) in enumerate(segments):
        nxt = proj(*segments[s + 1][:2]) if s + 1 < len(segments) else None
        emit(cur)
        cur = nxt


def _inproj(x2, g1, w_in, b_gate, l, tm):
    n = x2.shape[0]
    tok = lambda w: pl.BlockSpec((tm, w), lambda i: (i, 0))
    outs = [jax.ShapeDtypeStruct((n, NA_WIDTH), BF16)] * 3 + [jax.ShapeDtypeStruct((n, 2 * D_MODEL), BF16)]
    return pl.pallas_call(
        _inproj_kernel,
        grid=(n // tm,),
        in_specs=[tok(D_MODEL), _layer_spec(g1, l), _layer_spec(w_in, l), _layer_spec(b_gate, l)],
        out_specs=[tok(NA_WIDTH)] * 3 + [tok(2 * D_MODEL)],
        out_shape=outs,
        compiler_params=_cparams(("parallel",)),
        name="inproj",
    )(x2, g1, w_in, b_gate)


FINE_ROWS = 8


def _fproj_kernel(x_ref, g1_ref, w_ref, wch_ref, f_ref, t_ref, z_scr, zcat_scr):
    x = x_ref[...].reshape(FFT_RADIX * FINE_ROWS, D_MODEL)
    h = _rms(x, g1_ref[...]).astype(BF16)
    u = jnp.dot(h, w_ref[...], preferred_element_type=F32).astype(BF16)
    for g in range(FNET_GROUPS):
        sl = slice(g * FNET_GROUP_DIM, (g + 1) * FNET_GROUP_DIM)
        z = jnp.dot(u[:, sl], wch_ref[...], preferred_element_type=F32)
        z_scr[g] = z[:, :FNET_GROUP_DIM]
        z_scr[FNET_GROUPS + g] = z[:, FNET_GROUP_DIM:]
    for part in range(2):
        for bl in range(FINE_ROWS):
            for g in range(FNET_GROUPS):
                lo = bl * FNET_WIDTH + g * FNET_GROUP_DIM
                zcat_scr[part * FFT_RADIX:(part + 1) * FFT_RADIX, lo:lo + FNET_GROUP_DIM] = \
                    z_scr[part * FNET_GROUPS + g, pl.ds(bl, FFT_RADIX, stride=FINE_ROWS), :].astype(BF16)
    t_ref[...] = jnp.dot(f_ref[...], zcat_scr[...], preferred_element_type=F32).astype(BF16)


def _fproj(x4, g1, w_in, wch, f1, l):
    b, r, _, d = x4.shape
    nb = FINE_ROWS * FNET_WIDTH
    return pl.pallas_call(
        _fproj_kernel,
        grid=(b, r // FINE_ROWS),
        in_specs=[pl.BlockSpec((None, r, FINE_ROWS, d), lambda bi, j: (bi, 0, j, 0)),
                  _layer_spec(g1, l),
                  pl.BlockSpec((None, d, FNET_WIDTH), lambda bi, j: (l, 0, 0)),
                  _full_spec(wch), _full_spec(f1)],
        out_specs=pl.BlockSpec((None, 2 * r, nb), lambda bi, j: (bi, 0, j)),
        out_shape=jax.ShapeDtypeStruct((b, 2 * r, r * FNET_WIDTH), BF16),
        scratch_shapes=[pltpu.VMEM((2 * FNET_GROUPS, r * FINE_ROWS, FNET_GROUP_DIM), F32),
                        pltpu.VMEM((2 * r, nb), BF16)],
        compiler_params=_cparams(("parallel", "parallel")),
        name="fproj",
    )(x4, g1, w_in, wch, f1)


def _fft2_kernel(t_ref, g_ref, anchor_ref, y_ref, *, kb):
    del anchor_ref
    for kk in range(kb):
        rhs = jnp.concatenate([t_ref[0, kk], t_ref[1, kk]], axis=0)
        y_ref[kk] = jnp.dot(g_ref[kk], rhs, preferred_element_type=F32).astype(BF16)


def _fft2(t5, gtab, anchor, kb):
    b, _, r, _, c = t5.shape
    return pl.pallas_call(
        functools.partial(_fft2_kernel, kb=kb),
        grid=(b, r // kb),
        in_specs=[pl.BlockSpec((None, 2, kb, r, c), lambda bi, j: (bi, 0, j, 0, 0)),
                  pl.BlockSpec((kb, r, 2 * r), lambda bi, j: (j, 0, 0)),
                  pl.BlockSpec(memory_space=pl.ANY)],
        out_specs=pl.BlockSpec((None, kb, r, c), lambda bi, j: (bi, j, 0, 0)),
        out_shape=jax.ShapeDtypeStruct((b, r, r, c), BF16),
        compiler_params=_cparams(("parallel", "parallel")),
        name="fft2",
    )(t5, gtab, anchor)


@functools.lru_cache(maxsize=None)
def _dft_tables_np():
    r = FFT_RADIX
    idx = np.arange(r, dtype=np.int64)
    scale = r ** -0.5
    ang = ((idx[:, None] * idx[None, :]) % r) * (2.0 * math.pi / r)
    c, s = np.cos(ang) * scale, np.sin(ang) * scale
    wch = np.concatenate([c, -s], axis=1)
    f1 = np.concatenate([np.concatenate([c, s], axis=1),
                         np.concatenate([-s, c], axis=1)], axis=0)
    k1 = idx[:, None, None]
    k2 = idx[None, :, None]
    bb = idx[None, None, :]
    th = ((bb * (r * k2 + k1)) % (r * r)) * (2.0 * math.pi / (r * r))
    gtab = np.concatenate([np.cos(th) * scale, np.sin(th) * scale], axis=2)
    return tuple(a.astype(np.float32) for a in (wch, f1, gtab))


def _dft_tables():
    return tuple(jnp.asarray(a).astype(BF16) for a in _dft_tables_np())


def _kv_start_row(i, rows):
    return jnp.clip(i * QROWS - NA_ROW_WIN // 2, 0, rows - KV_ROWS)


CHAIN_LAG = 2


def _natten_kernel(q_ref, k_ref, v_ref, tile_ref, anchor_ref, o_ref, bias_ref, *, rows):
    del anchor_ref
    i = pl.program_id(0)

    @pl.when(i == 0)
    def _():
        top = NA_ROW_WIN - 1
        for dl in range(NA_ROW_WIN):
            for t in range(HEAD_PAIRS):
                for hh in range(2):
                    for w0 in range(0, NA_ROW_WIN, 2):
                        pair = [tile_ref[2 * t + hh, w + top - dl] for w in (w0, w0 + 1)]
                        bias_ref[dl, t, hh * GRID_W:(hh + 1) * GRID_W,
                                 w0 * GRID_W:(w0 + 2) * GRID_W] = jnp.concatenate(pair, axis=1)

    lane = lax.broadcasted_iota(jnp.int32, (ROW_TOKENS, PAIR_W), 1)
    lo_half = lane < NA_HEAD_DIM
    start = _kv_start_row(i, rows)

    deltas, offs = [], []
    for j in range(QROWS):
        r = i * QROWS + j
        rs = jnp.clip(r - NA_ROW_WIN // 2, 0, rows - NA_ROW_WIN)
        deltas.append(r - rs)
        offs.append(pl.multiple_of((rs - start) * ROW_TOKENS, ROW_TOKENS))

    def qrows(j):
        return slice(j * ROW_TOKENS, (j + 1) * ROW_TOKENS)

    def lanes(t):
        return slice(t * PAIR_W, (t + 1) * PAIR_W)

    def scores(t, j):
        q2 = q_ref[qrows(j), lanes(t)]
        zero = jnp.zeros_like(q2)
        qq = jnp.concatenate([jnp.where(lo_half, q2, zero),
                              jnp.where(lo_half, zero, q2)], axis=0)
        kw = k_ref[pl.ds(offs[j], WIN_TOKENS), lanes(t)]
        s = lax.dot_general(qq, kw, (((1,), (1,)), ((), ())), preferred_element_type=F32)
        return s + bias_ref[deltas[j], t]

    def softmax(s):
        m = jnp.max(s, axis=-1, keepdims=True)
        p = jnp.exp2(s - m)
        return p.astype(BF16), jnp.sum(p, axis=-1, keepdims=True)

    def weighted(t, j, p, l):
        vw = v_ref[pl.ds(offs[j], WIN_TOKENS), lanes(t)]
        o = jnp.dot(p, vw, preferred_element_type=F32) / l
        o2 = jnp.where(lo_half, o[:ROW_TOKENS], o[ROW_TOKENS:])
        o_ref[qrows(j), lanes(t)] = o2.astype(BF16)

    chains = [(t, j) for t in range(HEAD_PAIRS) for j in range(QROWS)]
    sc, sm = {}, {}
    for step in range(len(chains) + 2 * CHAIN_LAG):
        if step < len(chains):
            sc[step] = scores(*chains[step])
        n = step - CHAIN_LAG
        if 0 <= n < len(chains):
            sm[n] = softmax(sc.pop(n))
        n = step - 2 * CHAIN_LAG
        if 0 <= n < len(chains):
            weighted(*chains[n], *sm.pop(n))


def _natten(q, k, v, bias_tiles, l, bi, anchor):
    _, s, w = q.shape
    rows = s // GRID_W
    qspec = pl.BlockSpec((None, QBLOCK, w), lambda i: (bi, i, 0))
    kvspec = pl.BlockSpec((None, pl.Element(KV_TOKENS), pl.Element(w)),
                          lambda i: (bi, _kv_start_row(i, rows) * ROW_TOKENS, 0))
    return pl.pallas_call(
        functools.partial(_natten_kernel, rows=rows),
        grid=(rows // QROWS,),
        in_specs=[qspec, kvspec, kvspec, _layer_spec(bias_tiles, l), pl.BlockSpec(memory_space=pl.ANY)],
        out_specs=pl.BlockSpec((QBLOCK, w), lambda i: (i, 0)),
        out_shape=jax.ShapeDtypeStruct((s, w), BF16),
        scratch_shapes=[pltpu.VMEM((NA_ROW_WIN, HEAD_PAIRS, 2 * GRID_W, WIN_TOKENS), F32)],
        compiler_params=_cparams(("arbitrary",)),
        name="natten",
    )(q, k, v, bias_tiles, anchor)


@functools.lru_cache(maxsize=None)
def _bias_maps_np():
    cols = np.arange(GRID_W)
    col_start = np.clip(cols - NA_COL_WIN // 2, 0, GRID_W - NA_COL_WIN)
    kc = cols[None, :]
    valid = (kc >= col_start[:, None]) & (kc < col_start[:, None] + NA_COL_WIN)
    dc = kc - cols[:, None] + (NA_COL_WIN - 1)
    col_map = (dc[None] == np.arange(2 * NA_COL_WIN - 1)[:, None, None]) & valid[None]
    maskadd = np.where(valid, 0.0, MASK_VALUE)
    return col_map.astype(np.float32), maskadd.astype(np.float32)


def _bias_tiles(rpb):
    col_map, maskadd = _bias_maps_np()
    t = jnp.einsum('lhrd,dcx->lhrcx', rpb.astype(F32), jnp.asarray(col_map), precision=lax.Precision.HIGHEST)
    return t * LOG2E + jnp.asarray(maskadd)


def _merge_kernel(x_ref, y_ref, gate_ref, wfa_ref, wnb_ref, wout_ref, *rest, tiles_per_seq):
    att_refs, o_ref = rest[:-1], rest[-1]
    bi = pl.program_id(0) // tiles_per_seq
    att = att_refs[0][...]
    for b in range(1, len(att_refs)):
        att = jnp.where(bi == b, att_refs[b][...], att)
    fa = jnp.dot(y_ref[...], wfa_ref[...], preferred_element_type=F32)
    nb = jnp.dot(att, wnb_ref[...], preferred_element_type=F32)
    mixed = gate_ref[:, :D_MODEL].astype(F32) * fa + gate_ref[:, D_MODEL:].astype(F32) * nb
    o_ref[...] = x_ref[...] + jnp.dot(mixed.astype(BF16), wout_ref[...],
                                      preferred_element_type=F32)


def _merge(x2, y, atts, gates, w_fa, w_nb, w_out, l, tm):
    n = x2.shape[0]
    tps = atts[0].shape[0] // tm
    tok = lambda w: pl.BlockSpec((tm, w), lambda i: (i, 0))
    att_spec = lambda b: pl.BlockSpec((tm, NA_WIDTH), lambda i: (jnp.clip(i - b * tps, 0, tps - 1), 0))
    return pl.pallas_call(
        functools.partial(_merge_kernel, tiles_per_seq=tps),
        grid=(n // tm,),
        in_specs=[tok(D_MODEL), tok(FNET_WIDTH), tok(2 * D_MODEL),
                  _layer_spec(w_fa, l), _layer_spec(w_nb, l), _layer_spec(w_out, l)] +
                 [att_spec(b) for b in range(len(atts))],
        out_specs=tok(D_MODEL),
        out_shape=jax.ShapeDtypeStruct((n, D_MODEL), F32),
        compiler_params=_cparams(("parallel",)),
        name="merge",
    )(x2, y, gates, w_fa, w_nb, w_out, *atts)


HALO = 8
FF_SUB = 768
FF_GROUP = 2


def _ffn_kernel(xp_ref, xc_ref, xn_ref, g2_ref, wup_ref, cw_ref, cb_ref, wdn_ref, gf_ref,
                o_ref, *, tm, tiles_per_seq, final_norm):
    t = pl.program_id(0) % tiles_per_seq
    xc = xc_ref[...]
    xp = jnp.where(t > 0, xp_ref[...], 0.0)
    xn = jnp.where(t < tiles_per_seq - 1, xn_ref[...], 0.0)
    xin = jnp.concatenate([xp, xc, xn], axis=0)
    h = _rms(xin, g2_ref[...]).astype(BF16)
    bounds = list(range(0, D_FF, FF_SUB)) + [D_FF]
    subs = list(zip(bounds[:-1], bounds[1:]))

    def up(c):
        lo, hi = subs[c]
        return [jnp.dot(h, wup_ref[:, o + lo:o + hi], preferred_element_type=F32) for o in (0, D_FF)]

    def conv(u, lo, hi):
        shifted = (pltpu.roll(u, 1, 0), u, pltpu.roll(u, tm + 2 * HALO - 1, 0))
        out = cb_ref[:, lo:hi]
        for tap, arr in enumerate(shifted):
            out = out + arr[HALO:HALO + tm] * cw_ref[tap:tap + 1, lo:hi]
        return out

    acc = xc
    acts = []
    ups = up(0)
    for c, (lo, hi) in enumerate(subs):
        nxt = up(c + 1) if c + 1 < len(subs) else None
        val = conv(ups[0], lo, hi)
        gate = conv(ups[1], D_FF + lo, D_FF + hi)
        acts.append((val * (gate * _sigmoid(gate))).astype(BF16))
        if len(acts) == FF_GROUP or c == len(subs) - 1:
            glo = subs[c + 1 - len(acts)][0]
            acc = acc + jnp.dot(jnp.concatenate(acts, axis=1), wdn_ref[glo:hi, :],
                                preferred_element_type=F32)
            acts = []
        ups = nxt
    if final_norm:
        acc = _rms(acc, gf_ref[...])
    o_ref[...] = acc


def _ffn(x2, g2, w_up, conv_w, conv_b, w_down, gf, l, tm, seq, final_norm):
    n = x2.shape[0]
    hb = tm // HALO
    nhalo = n // HALO
    return pl.pallas_call(
        functools.partial(_ffn_kernel, tm=tm, tiles_per_seq=seq // tm, final_norm=final_norm),
        grid=(n // tm,),
        in_specs=[pl.BlockSpec((HALO, D_MODEL), lambda i: (jnp.maximum(i * hb - 1, 0), 0)),
                  pl.BlockSpec((tm, D_MODEL), lambda i: (i, 0)),
                  pl.BlockSpec((HALO, D_MODEL), lambda i: (jnp.minimum((i + 1) * hb, nhalo - 1), 0)),
                  _layer_spec(g2, l), _layer_spec(w_up, l), _layer_spec(conv_w, l),
                  _layer_spec(conv_b, l), _layer_spec(w_down, l), _full_spec(gf)],
        out_specs=pl.BlockSpec((tm, D_MODEL), lambda i: (i, 0)),
        out_shape=jax.ShapeDtypeStruct((n, D_MODEL), F32),
        compiler_params=_cparams(("parallel",)),
        name="ffn",
    )(x2, x2, x2, g2, w_up, conv_w, conv_b, w_down, gf)


def kernel(x, norm1_g, w_in, b_gate, rpb, w_fa, w_nb, w_out, norm2_g, w_up, conv_w, conv_b,
           w_down, final_g):
    b, s, d = x.shape
    depth = w_in.shape[0]
    r = FFT_RADIX
    assert s == r * r and d == D_MODEL and s % (GRID_W * QROWS) == 0
    wch, f1, gtab = _dft_tables()
    bias_tab = _bias_tiles(rpb)
    g1, g2 = norm1_g.reshape(depth, 1, d), norm2_g.reshape(depth, 1, d)
    cb = conv_b.reshape(depth, 1, 2 * D_FF)
    gf = final_g.reshape(1, d)
    w_in, w_fa, w_nb, w_out, w_up, w_down = (
        w.astype(BF16) for w in (w_in, w_fa, w_nb, w_out, w_up, w_down))
    x2 = x.reshape(b * s, d)
    for l in range(depth):
        q, k, v, gates = _inproj(x2, g1, w_in, b_gate, l, tm=512)
        q, k, v = (a.reshape(b, s, NA_WIDTH) for a in (q, k, v))
        t = _fproj(x2.reshape(b, r, r, d), g1, w_in, wch, f1, l)
        atts = [_natten(q, k, v, bias_tab, l, bi, t) for bi in range(b - 1)]
        yt = _fft2(t.reshape(b, 2, r, r, FNET_WIDTH), gtab, atts[-1] if atts else t, kb=8)
        atts.append(_natten(q, k, v, bias_tab, l, b - 1, yt))
        y = jnp.transpose(yt, (0, 2, 1, 3)).reshape(b * s, FNET_WIDTH)
        x2 = _merge(x2, y, atts, gates, w_fa, w_nb, w_out, l, tm=512)
        x2 = _ffn(x2, g2, w_up, conv_w, cb, w_down, gf, l,
                  tm=512, seq=s, final_norm=(l == depth - 1))
    return x2.reshape(b, s, d)
```
